```python
import jax, jax.numpy as jnp
from jax import lax
import numpy as np

D_MODEL = 1024
BATCH = 8
SEQ = 4096
DEPTH = 1

SSM_EXPAND = 2
SSM_D_INNER = SSM_EXPAND * D_MODEL
SSM_HEAD_DIM = 64
SSM_N_HEADS = SSM_D_INNER // SSM_HEAD_DIM
SSM_N_GROUPS = 8
SSM_D_STATE = 128
SSM_CHUNK = 128
SSM_CONV_DIM = SSM_D_INNER + 2 * SSM_N_GROUPS * SSM_D_STATE
GDN_HEAD_K = 128
GDN_HEAD_V = 128
GDN_N_QK_HEADS = D_MODEL // GDN_HEAD_K
GDN_N_V_HEADS = 2 * GDN_N_QK_HEADS
GDN_KEY_DIM = GDN_N_QK_HEADS * GDN_HEAD_K
GDN_VAL_DIM = GDN_N_V_HEADS * GDN_HEAD_V
GDN_CHUNK = 64
GDN_CONV_DIM = 2 * GDN_KEY_DIM + GDN_VAL_DIM
CONV_K = 4
MLP_HIDDEN = 4 * D_MODEL
EPS = 1e-6
IN_SPLIT_SIZES = (SSM_D_INNER, SSM_CONV_DIM, SSM_N_HEADS, GDN_CONV_DIM, GDN_VAL_DIM,
                  GDN_N_V_HEADS, GDN_N_V_HEADS, D_MODEL, D_MODEL)
IN_PROJ_DIM = sum(IN_SPLIT_SIZES)

kernel_name = "hybrid_ssd_gdn_sandwich_adaln_block"


def rmsnorm(x, w):
    xf = x.astype(jnp.float32)
    y = xf * lax.rsqrt(jnp.mean(xf * xf, axis=-1, keepdims=True) + EPS)
    return (y * w.astype(jnp.float32)).astype(x.dtype)


def l2norm(x):
    return x * lax.rsqrt(jnp.sum(x * x, axis=-1, keepdims=True) + EPS)


def causal_depthwise_conv(x, w):
    return lax.conv_general_dilated(
        x, w[:, None, :].astype(x.dtype), window_strides=(1,), padding=[(CONV_K - 1, 0)],
        dimension_numbers=('NWC', 'WIO', 'NWC'), feature_group_count=x.shape[-1])


def ssd_chunked_scan(xh, dt, A, Bm, Cm):
    Bsz, S, H, P = xh.shape
    G, N = Bm.shape[-2:]
    hg = H // G
    L = SSM_CHUNK
    nc = S // L
    xdt = jnp.moveaxis((xh * dt[..., None]).reshape(Bsz, nc, L, G, hg, P), 1, 0)
    a = jnp.moveaxis((dt * A).reshape(Bsz, nc, L, G, hg), 1, 0)
    Bc = jnp.moveaxis(Bm.reshape(Bsz, nc, L, G, N), 1, 0)
    Cc = jnp.moveaxis(Cm.reshape(Bsz, nc, L, G, N), 1, 0)
    causal = jnp.tril(jnp.ones((L, L), dtype=bool))[None, :, :, None, None]

    def step(state, inp):
        xc, ac, bc, cc = inp
        acum = jnp.cumsum(ac, axis=1)
        seg = acum[:, :, None] - acum[:, None, :]
        decay = jnp.exp(jnp.where(causal, seg, -jnp.inf))
        cb = jnp.einsum('blgn,bsgn->blsg', cc, bc)
        y_diag = jnp.einsum('blsg,blsgh,bsghp->blghp', cb, decay, xc)
        y_off = jnp.einsum('blgn,bghpn->blghp', cc, state) * jnp.exp(acum)[..., None]
        a_last = acum[:, -1]
        w_s = jnp.exp(a_last[:, None] - acum)
        new_state = state * jnp.exp(a_last)[..., None, None] + jnp.einsum(
            'bsgn,bsgh,bsghp->bghpn', bc, w_s, xc)
        return new_state, y_diag + y_off

    state0 = jnp.zeros((Bsz, G, hg, P, N), dtype=jnp.float32)
    _, y = lax.scan(step, state0, (xdt, a, Bc, Cc))
    return jnp.moveaxis(y, 0, 1).reshape(Bsz, S, H, P)


def gated_delta_rule_chunked(q, k, v, g, beta):
    Bsz, S, H, dk = q.shape
    dv = v.shape[-1]
    L = GDN_CHUNK
    nc = S // L
    q = q * (dk ** -0.5)

    def chunks(t):
        t = t.reshape((Bsz, nc, L, H) + t.shape[3:])
        return jnp.moveaxis(t, (1, 3), (0, 2))

    causal = jnp.tril(jnp.ones((L, L), dtype=bool))
    strict = jnp.tril(jnp.ones((L, L), dtype=bool), -1)
    eye = jnp.eye(L, dtype=jnp.float32)

    def step(state, inp):
        qc, kc, vc, gc, bc = inp
        gcum = jnp.cumsum(gc, axis=-1)
        dmat = jnp.exp(jnp.where(causal, gcum[..., :, None] - gcum[..., None, :], -jnp.inf))
        kb = kc * bc[..., None]
        a_low = jnp.where(strict, jnp.einsum('bhid,bhjd->bhij', kb, kc) * dmat, 0.0)
        rhs = jnp.concatenate([vc * bc[..., None], kb * jnp.exp(gcum)[..., None]], axis=-1)
        sol = lax.linalg.triangular_solve(eye + a_low, rhs, left_side=True, lower=True,
                                          unit_diagonal=True)
        u, w = sol[..., :dv], sol[..., dv:]
        attn = jnp.einsum('bhid,bhjd->bhij', qc, kc) * dmat
        v_new = u - jnp.einsum('bhlk,bhkv->bhlv', w, state)
        o = jnp.einsum('bhlk,bhkv->bhlv', qc * jnp.exp(gcum)[..., None], state) + jnp.einsum(
            'bhij,bhjv->bhiv', attn, v_new)
        g_last = gcum[..., -1]
        k_dec = kc * jnp.exp(g_last[..., None] - gcum)[..., None]
        new_state = state * jnp.exp(g_last)[..., None, None] + jnp.einsum(
            'bhlk,bhlv->bhkv', k_dec, v_new)
        return new_state, o

    state0 = jnp.zeros((Bsz, H, dk, dv), dtype=jnp.float32)
    _, o = lax.scan(step, state0, (chunks(q), chunks(k), chunks(v), chunks(g), chunks(beta)))
    return jnp.moveaxis(o, (0, 2), (1, 3)).reshape(Bsz, S, H, dv)


def mamba2_branch(z, xbc, dt_raw, conv_w, conv_b, dt_bias, A_log, d_skip, norm_w):
    f32 = jnp.float32
    Bsz, S, _ = xbc.shape
    xbc = jax.nn.silu(causal_depthwise_conv(xbc, conv_w) + conv_b)
    xs, Bm, Cm = jnp.split(xbc, [SSM_D_INNER, SSM_D_INNER + SSM_N_GROUPS * SSM_D_STATE], axis=-1)
    xh = xs.reshape(Bsz, S, SSM_N_HEADS, SSM_HEAD_DIM).astype(f32)
    dt = jax.nn.softplus(dt_raw.astype(f32) + dt_bias.astype(f32))
    A = -jnp.exp(A_log.astype(f32))
    y = ssd_chunked_scan(xh, dt, A,
                         Bm.reshape(Bsz, S, SSM_N_GROUPS, SSM_D_STATE).astype(f32),
                         Cm.reshape(Bsz, S, SSM_N_GROUPS, SSM_D_STATE).astype(f32))
    y = y + d_skip.astype(f32)[:, None] * xh
    y = y.reshape(Bsz, S, SSM_D_INNER) * jax.nn.silu(z.astype(f32))
    y = rmsnorm(y.reshape(Bsz, S, SSM_N_GROUPS, -1), norm_w.reshape(SSM_N_GROUPS, -1))
    return y.reshape(Bsz, S, SSM_D_INNER).astype(z.dtype)


def gated_deltanet_branch(qkv, z, b, a, conv_w, dt_bias, A_log, norm_w):
    f32 = jnp.float32
    Bsz, S, _ = qkv.shape
    qkv = jax.nn.silu(causal_depthwise_conv(qkv, conv_w))
    q, k, v = jnp.split(qkv, [GDN_KEY_DIM, 2 * GDN_KEY_DIM], axis=-1)
    rep = GDN_N_V_HEADS // GDN_N_QK_HEADS
    q = jnp.repeat(l2norm(q.reshape(Bsz, S, GDN_N_QK_HEADS, GDN_HEAD_K).astype(f32)), rep, axis=2)
    k = jnp.repeat(l2norm(k.reshape(Bsz, S, GDN_N_QK_HEADS, GDN_HEAD_K).astype(f32)), rep, axis=2)
    v = v.reshape(Bsz, S, GDN_N_V_HEADS, GDN_HEAD_V).astype(f32)
    beta = jax.nn.sigmoid(b.astype(f32))
    g = -jnp.exp(A_log.astype(f32)) * jax.nn.softplus(a.astype(f32) + dt_bias.astype(f32))
    o = gated_delta_rule_chunked(q, k, v, g, beta)
    o = rmsnorm(o, norm_w) * jax.nn.silu(z.reshape(Bsz, S, GDN_N_V_HEADS, GDN_HEAD_V).astype(f32))
    return o.reshape(Bsz, S, GDN_VAL_DIM).astype(z.dtype)


def setup_inputs(seed: int = 0) -> dict:
    key = jax.random.key(seed)
    ks = jax.random.split(key, 26)
    f32 = jnp.float32
    nrm = lambda k, shape, scale: jax.random.normal(k, shape, f32) * scale
    gain = lambda k, shape: 1.0 + 0.02 * jax.random.normal(k, shape, f32)

    def inv_softplus_dt(k, shape):
        dt = jnp.exp(jax.random.uniform(k, shape, f32, np.log(1e-3), np.log(1e-1)))
        return dt + jnp.log(-jnp.expm1(-dt))

    Dm = D_MODEL
    return {
        'x': jax.random.normal(ks[0], (BATCH, SEQ, Dm), f32),
        'c': jax.random.normal(ks[1], (BATCH, Dm), f32),
        'w_ada': nrm(ks[2], (DEPTH, Dm, 6 * Dm), 0.5 * Dm ** -0.5),
        'b_ada': nrm(ks[3], (DEPTH, 6 * Dm), 0.02),
        'norm_mix_pre': gain(ks[4], (DEPTH, Dm)),
        'norm_mix_post': gain(ks[5], (DEPTH, Dm)),
        'w_in': nrm(ks[6], (DEPTH, Dm, IN_PROJ_DIM), Dm ** -0.5),
        'ssm_conv_w': nrm(ks[7], (DEPTH, CONV_K, SSM_CONV_DIM), CONV_K ** -0.5),
        'ssm_conv_b': nrm(ks[8], (DEPTH, SSM_CONV_DIM), 0.02),
        'ssm_dt_bias': inv_softplus_dt(ks[9], (DEPTH, SSM_N_HEADS)),
        'ssm_A_log': jnp.log(jax.random.uniform(ks[10], (DEPTH, SSM_N_HEADS), f32, 1.0, 16.0)),
        'ssm_D': gain(ks[11], (DEPTH, SSM_N_HEADS)),
        'ssm_norm_w': gain(ks[12], (DEPTH, SSM_D_INNER)),
        'gdn_conv_w': nrm(ks[13], (DEPTH, CONV_K, GDN_CONV_DIM), CONV_K ** -0.5),
        'gdn_dt_bias': inv_softplus_dt(ks[14], (DEPTH, GDN_N_V_HEADS)),
        'gdn_A_log': jnp.log(jax.random.uniform(ks[15], (DEPTH, GDN_N_V_HEADS), f32, 1.0, 16.0)),
        'gdn_norm_w': gain(ks[16], (DEPTH, GDN_HEAD_V)),
        'w_ssm_up': nrm(ks[17], (DEPTH, SSM_D_INNER, Dm), SSM_D_INNER ** -0.5),
        'w_gdn_up': nrm(ks[18], (DEPTH, GDN_VAL_DIM, Dm), GDN_VAL_DIM ** -0.5),
        'w_out': nrm(ks[19], (DEPTH, Dm, Dm), Dm ** -0.5),
        'norm_mlp_pre': gain(ks[20], (DEPTH, Dm)),
        'norm_mlp_post': gain(ks[21], (DEPTH, Dm)),
        'w_mlp_up': nrm(ks[22], (DEPTH, Dm, MLP_HIDDEN), Dm ** -0.5),
        'w_mlp_down': nrm(ks[23], (DEPTH, MLP_HIDDEN, Dm), MLP_HIDDEN ** -0.5),
    }


def reference(x, c, w_ada, b_ada, norm_mix_pre, norm_mix_post, w_in, ssm_conv_w, ssm_conv_b,
              ssm_dt_bias, ssm_A_log, ssm_D, ssm_norm_w, gdn_conv_w, gdn_dt_bias, gdn_A_log,
              gdn_norm_w, w_ssm_up, w_gdn_up, w_out, norm_mlp_pre, norm_mlp_post, w_mlp_up,
              w_mlp_down):
    offsets = [int(o) for o in np.cumsum(IN_SPLIT_SIZES)[:-1]]
    c_act = jax.nn.silu(c)
    for l in range(DEPTH):
        mod = c_act @ w_ada[l] + b_ada[l]
        sh1, sc1, g1, sh2, sc2, g2 = [m[:, None, :] for m in jnp.split(mod, 6, axis=-1)]

        h = rmsnorm(x, norm_mix_pre[l]) * (1.0 + sc1) + sh1
        proj = h @ w_in[l]
        (z_ssm, xbc, dt_raw, qkv, z_gdn, b_gdn, a_gdn,
         gate_ssm, gate_gdn) = jnp.split(proj, offsets, axis=-1)
        y_ssm = mamba2_branch(z_ssm, xbc, dt_raw, ssm_conv_w[l], ssm_conv_b[l], ssm_dt_bias[l],
                              ssm_A_log[l], ssm_D[l], ssm_norm_w[l]) @ w_ssm_up[l]
        y_gdn = gated_deltanet_branch(qkv, z_gdn, b_gdn, a_gdn, gdn_conv_w[l], gdn_dt_bias[l],
                                      gdn_A_log[l], gdn_norm_w[l]) @ w_gdn_up[l]
        merged = jax.nn.sigmoid(gate_ssm) * y_ssm + jax.nn.sigmoid(gate_gdn) * y_gdn
        x = x + g1 * rmsnorm(merged @ w_out[l], norm_mix_post[l])

        h = rmsnorm(x, norm_mlp_pre[l]) * (1.0 + sc2) + sh2
        y = jnp.square(jax.nn.relu(h @ w_mlp_up[l])) @ w_mlp_down[l]
        x = x + g2 * rmsnorm(y, norm_mlp_post[l])
    return x
```

```python
import functools

import jax
import jax.numpy as jnp
from jax import lax
from jax.experimental import pallas as pl
from jax.experimental.pallas import tpu as pltpu

F32 = jnp.float32
BF16 = jnp.bfloat16

EPS = 1e-6
CONV_K = 4
LANES = 128
NEG_BIG = -1e30

SSM_HEAD_DIM = 64
SSM_N_GROUPS = 8
SSM_D_STATE = 128
SSM_CHUNK = 128
GDN_HEAD = 128
GDN_CHUNK = 64

VMEM_LIMIT = 56 * 1024 * 1024


def _params(sem, vmem=VMEM_LIMIT):
    return pltpu.CompilerParams(dimension_semantics=sem, vmem_limit_bytes=vmem)


def _const_spec(shape):
    nd = len(shape)
    return pl.BlockSpec(shape, lambda *_: (0,) * nd, pipeline_mode=pl.Buffered(1))


def _dot(a, b):
    return jnp.dot(a, b, preferred_element_type=F32)


def _dot_nt(a, b):
    return lax.dot_general(a, b, (((1,), (1,)), ((), ())), preferred_element_type=F32)


def _dot_tn(a, b):
    return lax.dot_general(a, b, (((0,), (0,)), ((), ())), preferred_element_type=F32)


def _split(x, n):
    parts = []
    r = x
    for i in range(n):
        p = r.astype(BF16)
        parts.append(p)
        if i + 1 < n:
            r = r - p.astype(F32)
    return parts


def _sel_left(m01, x, n=3):
    out = None
    for p in reversed(_split(x, n)):
        t = _dot(m01, p)
        out = t if out is None else out + t
    return out


def _sel_right(x, m01, n=3):
    out = None
    for p in reversed(_split(x, n)):
        t = _dot(p, m01)
        out = t if out is None else out + t
    return out


def _silu(x):
    return x * jax.nn.sigmoid(x)


def _softplus(x):
    return jnp.maximum(x, 0.0) + jnp.log1p(jnp.exp(-jnp.abs(x)))


def _rms_rows(x, w):
    return x * lax.rsqrt(jnp.mean(x * x, axis=-1, keepdims=True) + EPS) * w


def _tri_mask(n, strict=False):
    r = lax.broadcasted_iota(jnp.int32, (n, n), 0)
    c = lax.broadcasted_iota(jnp.int32, (n, n), 1)
    return (r > c) if strict else (r >= c)


def _conv_silu(hist_ref, x_ref, w_ref, bias, rows):
    hist_ref[pl.ds(8, rows), :] = x_ref[...].astype(F32)
    acc = bias
    for k in range(CONV_K):
        acc = acc + w_ref[k:k + 1, :] * hist_ref[pl.ds(8 - (CONV_K - 1) + k, rows), :]
    hist_ref[pl.ds(0, 8), :] = hist_ref[pl.ds(rows, 8), :]
    return _silu(acc)


def _mod_kernel(c_ref, w_ref, b_ref, o_ref):
    ca = _silu(c_ref[...]).astype(BF16)
    o_ref[...] = _dot(ca, w_ref[...].astype(BF16)) + b_ref[...]


def _mod_call(c, w_ada, b_ada):
    bsz, d = c.shape
    n = w_ada.shape[1]
    return pl.pallas_call(
        _mod_kernel,
        grid=(n // d,),
        in_specs=[pl.BlockSpec((bsz, d), lambda j: (0, 0)),
                  pl.BlockSpec((d, d), lambda j: (0, j)),
                  pl.BlockSpec((1, d), lambda j: (0, j))],
        out_specs=pl.BlockSpec((bsz, d), lambda j: (0, j)),
        out_shape=jax.ShapeDtypeStruct((bsz, n), F32),
        compiler_params=_params(("arbitrary",)),
        name="mod",
    )(c, w_ada, b_ada.reshape(1, n))


IN_PROJ_COL_CHUNK = 512


def _inproj_kernel(x_ref, mod_ref, nw_ref, *refs):
    n_seg = len(refs) // 2
    w_refs, o_refs = refs[:n_seg], refs[n_seg:]
    h = _rms_rows(x_ref[...], nw_ref[...]) * (1.0 + mod_ref[1:2, :]) + mod_ref[0:1, :]
    hb = h.astype(BF16)
    for w_ref, o_ref in zip(w_refs, o_refs):
        width = w_ref.shape[1]
        step = min(IN_PROJ_COL_CHUNK, width)
        for c0 in range(0, width, step):
            o_ref[:, c0:c0 + step] = _dot(hb, w_ref[:, c0:c0 + step]).astype(o_ref.dtype)


def _inproj_call(x, mod3, norm_w, seg_ws, seg_dtypes, tm):
    bsz, s, d = x.shape
    in_specs = [pl.BlockSpec((None, tm, d), lambda b, i: (b, i, 0)),
                pl.BlockSpec((None, 6, d), lambda b, i: (b, 0, 0)),
                _const_spec((1, d))]
    in_specs += [_const_spec(w.shape) for w in seg_ws]
    out_specs = [pl.BlockSpec((None, tm, w.shape[1]), lambda b, i: (b, i, 0)) for w in seg_ws]
    out_shape = [jax.ShapeDtypeStruct((bsz, s, w.shape[1]), dt) for w, dt in zip(seg_ws, seg_dtypes)]
    return pl.pallas_call(
        _inproj_kernel,
        grid=(bsz, s // tm),
        in_specs=in_specs,
        out_specs=out_specs,
        out_shape=out_shape,
        compiler_params=_params(("parallel", "parallel")),
        name="in_proj",
    )(x, mod3, norm_w.reshape(1, d), *seg_ws)


def _ssd_kernel(z_ref, xbc_ref, sm_ref, cw_ref, cb_ref, dtb_ref, alog_ref, dskip_ref, nw_ref,
                e_ref, o_ref, hist_ref, state_ref, *, d_inner):
    L = SSM_CHUNK
    G = SSM_N_GROUPS
    N = SSM_D_STATE
    gw = d_inner // G
    hg = gw // SSM_HEAD_DIM

    @pl.when(pl.program_id(1) == 0)
    def _():
        hist_ref[pl.ds(0, 8), :] = jnp.zeros((8, hist_ref.shape[1]), F32)
        state_ref[...] = jnp.zeros_like(state_ref)

    xbc = _conv_silu(hist_ref, xbc_ref, cw_ref, cb_ref[...], L)
    xs = xbc[:, :d_inner]

    dt = _softplus(sm_ref[...] + dtb_ref[...])
    a = dt * (-jnp.exp(alog_ref[...]))
    tri = _tri_mask(L).astype(BF16)
    acum = _sel_left(tri, a)
    acum_t = acum.T
    a_last = acum[L - 1:L, :]
    e = e_ref[...]
    dt_e = _sel_right(dt, e, 2)
    eac_e = _sel_right(jnp.exp(acum), e, 2)
    ws_e = _sel_right(jnp.exp(a_last - acum), e, 2)
    ea_last_e = eac_e[L - 1:L, :]

    xdt = xs * dt_e
    xdt_b = xdt.astype(BF16)
    xw_b = (xdt * ws_e).astype(BF16)
    causal = _tri_mask(L)
    lane_head = lax.broadcasted_iota(jnp.int32, (L, gw), 1) // SSM_HEAD_DIM

    for g in range(G):
        cs = slice(g * gw, (g + 1) * gw)
        b_g = xbc[:, d_inner + g * N: d_inner + (g + 1) * N].astype(BF16)
        c_g = xbc[:, d_inner + G * N + g * N: d_inner + G * N + (g + 1) * N].astype(BF16)
        cb = _dot_nt(c_g, b_g)
        xdt_g = xdt_b[:, cs]
        m_parts, x_parts = [], []
        for j in range(hg):
            h = g * hg + j
            seg = acum[:, h:h + 1] - acum_t[h:h + 1, :]
            decay = jnp.exp(jnp.where(causal, seg, NEG_BIG))
            m_parts.append((cb * decay).astype(BF16))
            x_parts.append(jnp.where(lane_head == j, xdt_g, jnp.zeros_like(xdt_g)))
        y_diag = _dot(jnp.concatenate(m_parts, axis=1), jnp.concatenate(x_parts, axis=0))
        st = state_ref[g]
        y_off = _dot(c_g, st.astype(BF16)) * eac_e[:, cs]
        state_ref[g] = st * ea_last_e[:, cs] + _dot_tn(b_g, xw_b[:, cs])
        y = y_diag + y_off + dskip_ref[:, cs] * xs[:, cs]
        y = y * _silu(z_ref[:, cs].astype(F32))
        o_ref[:, cs] = _rms_rows(y, nw_ref[:, cs]).astype(o_ref.dtype)


def _ssd_call(z, xbc, small, conv_w, conv_b, dt_bias, a_log, d_skip, norm_w):
    bsz, s, d_inner = z.shape
    conv_dim = xbc.shape[-1]
    n_heads = dt_bias.shape[0]
    L = SSM_CHUNK
    gw = d_inner // SSM_N_GROUPS
    pad = lambda v: jnp.pad(v.astype(F32), (0, LANES - n_heads)).reshape(1, LANES)
    expand = (jnp.arange(LANES)[:, None] == (jnp.arange(d_inner) // SSM_HEAD_DIM)[None, :]).astype(BF16)
    d_e = jnp.repeat(d_skip.astype(F32), SSM_HEAD_DIM).reshape(1, d_inner)
    blk = lambda w: pl.BlockSpec((None, L, w), lambda b, c: (b, c, 0))
    return pl.pallas_call(
        functools.partial(_ssd_kernel, d_inner=d_inner),
        grid=(bsz, s // L),
        in_specs=[blk(d_inner), blk(conv_dim), blk(LANES),
                  _const_spec((CONV_K, conv_dim)), _const_spec((1, conv_dim)),
                  _const_spec((1, LANES)), _const_spec((1, LANES)),
                  _const_spec((1, d_inner)), _const_spec((1, d_inner)),
                  _const_spec((LANES, d_inner))],
        out_specs=blk(d_inner),
        out_shape=jax.ShapeDtypeStruct((bsz, s, d_inner), BF16),
        scratch_shapes=[pltpu.VMEM((8 + L, conv_dim), F32),
                        pltpu.VMEM((SSM_N_GROUPS, SSM_D_STATE, gw), F32)],
        compiler_params=_params(("parallel", "arbitrary")),
        name="ssd",
    )(z, xbc, small, conv_w.astype(F32), conv_b.astype(F32).reshape(1, conv_dim),
      pad(dt_bias), pad(a_log), d_e, norm_w.astype(F32).reshape(1, d_inner), expand)


def _unit_lower_inverse(a_strict, n):
    r = lax.broadcasted_iota(jnp.int32, (n, n), 0)
    c = lax.broadcasted_iota(jnp.int32, (n, n), 1)
    eye = (r == c).astype(F32)
    inv = eye
    b = 1
    while b < n:
        off = ((r // (2 * b)) == (c // (2 * b))) & ((r // b) % 2 == 1) & ((c // b) % 2 == 0)
        a_off = jnp.where(off, a_strict, 0.0)
        if b == 1:
            inv = eye - a_off
        else:
            t = jnp.dot(inv, a_off, preferred_element_type=F32, precision=lax.Precision.HIGHEST)
            inv = inv - jnp.dot(t, inv, preferred_element_type=F32, precision=lax.Precision.HIGHEST)
        b *= 2
    return inv


def _gdn_kernel(qkv_ref, z_ref, sm_ref, cw_ref, dtb_ref, alog_ref, nw_ref, eb_ref, eg_ref,
                o_ref, hist_ref, state_ref, *, key_dim, b_lane, a_lane):
    L = GDN_CHUNK
    dh = GDN_HEAD
    n_qk = key_dim // dh
    n_v = (qkv_ref.shape[1] - 2 * key_dim) // dh
    rep = n_v // n_qk

    @pl.when(pl.program_id(1) == 0)
    def _():
        hist_ref[pl.ds(0, 8), :] = jnp.zeros((8, hist_ref.shape[1]), F32)
        state_ref[...] = jnp.zeros_like(state_ref)

    qkv = _conv_silu(hist_ref, qkv_ref, cw_ref, 0.0, L)

    sm = sm_ref[...]
    beta = jax.nn.sigmoid(sm)
    gl = -jnp.exp(alog_ref[...]) * _softplus(sm + dtb_ref[...])
    tri = _tri_mask(L).astype(BF16)
    gcum = _sel_left(tri, gl)
    gcum_t = gcum.T
    g_last = gcum[L - 1:L, :]
    beta_e = _sel_right(beta, eb_ref[...], 2)
    egc_e = _sel_right(jnp.exp(gcum), eg_ref[...], 2)
    kdec_e = _sel_right(jnp.exp(g_last - gcum), eg_ref[...], 2)

    causal = _tri_mask(L)
    strict = _tri_mask(L, strict=True)
    scale = dh ** -0.5

    for j in range(n_qk):
        q = qkv[:, j * dh:(j + 1) * dh]
        k = qkv[:, key_dim + j * dh: key_dim + (j + 1) * dh]
        q = q * lax.rsqrt(jnp.sum(q * q, axis=-1, keepdims=True) + EPS) * scale
        k = k * lax.rsqrt(jnp.sum(k * k, axis=-1, keepdims=True) + EPS)
        k_b = k.astype(BF16)
        kk = _dot_nt(k_b, k_b)
        qk = _dot_nt(q.astype(BF16), k_b)
        for i in range(rep):
            h = j * rep + i
            hs = slice(h * dh, (h + 1) * dh)
            v = qkv[:, 2 * key_dim + h * dh: 2 * key_dim + (h + 1) * dh]
            seg = gcum[:, a_lane + h:a_lane + h + 1] - gcum_t[a_lane + h:a_lane + h + 1, :]
            dmat = jnp.exp(jnp.where(causal, seg, NEG_BIG))
            a_low = jnp.where(strict, kk * dmat, 0.0) * beta[:, b_lane + h:b_lane + h + 1]
            t_inv = _unit_lower_inverse(a_low, L)
            be = beta_e[:, hs]
            rhs = jnp.concatenate([v * be, k * (be * egc_e[:, hs])], axis=1)
            sol = _dot(t_inv.astype(BF16), rhs.astype(BF16))
            u, w = sol[:, :dh], sol[:, dh:]
            st = state_ref[h]
            st_b = st.astype(BF16)
            v_new = u - _dot(w.astype(BF16), st_b)
            v_new_b = v_new.astype(BF16)
            o = _dot((q * egc_e[:, hs]).astype(BF16), st_b) + _dot((qk * dmat).astype(BF16), v_new_b)
            state_ref[h] = st * egc_e[L - 1:L, hs] + _dot_tn((k * kdec_e[:, hs]).astype(BF16), v_new_b)
            o = _rms_rows(o, nw_ref[...]) * _silu(z_ref[:, hs].astype(F32))
            o_ref[:, hs] = o.astype(o_ref.dtype)


def _gdn_call(qkv, z, small, conv_w, dt_bias, a_log, norm_w, b_lane, a_lane):
    bsz, s, conv_dim = qkv.shape
    val_dim = z.shape[-1]
    key_dim = (conv_dim - val_dim) // 2
    n_v = val_dim // GDN_HEAD
    L = GDN_CHUNK
    place = lambda v, at: jnp.pad(v.astype(F32), (at, LANES - at - n_v)).reshape(1, LANES)
    head_of = (jnp.arange(val_dim) // GDN_HEAD)[None, :]
    lane = jnp.arange(LANES)[:, None]
    e_b = (lane == head_of + b_lane).astype(BF16)
    e_g = (lane == head_of + a_lane).astype(BF16)
    blk = lambda w: pl.BlockSpec((None, L, w), lambda b, c: (b, c, 0))
    return pl.pallas_call(
        functools.partial(_gdn_kernel, key_dim=key_dim, b_lane=b_lane, a_lane=a_lane),
        grid=(bsz, s // L),
        in_specs=[blk(conv_dim), blk(val_dim), blk(LANES),
                  _const_spec((CONV_K, conv_dim)),
                  _const_spec((1, LANES)), _const_spec((1, LANES)), _const_spec((1, GDN_HEAD)),
                  _const_spec((LANES, val_dim)), _const_spec((LANES, val_dim))],
        out_specs=blk(val_dim),
        out_shape=jax.ShapeDtypeStruct((bsz, s, val_dim), BF16),
        scratch_shapes=[pltpu.VMEM((8 + L, conv_dim), F32),
                        pltpu.VMEM((n_v, GDN_HEAD, GDN_HEAD), F32)],
        compiler_params=_params(("parallel", "arbitrary")),
        name="gdn",
    )(qkv, z, small, conv_w.astype(F32), place(dt_bias, a_lane), place(a_log, a_lane),
      norm_w.astype(F32).reshape(1, GDN_HEAD), e_b, e_g)


def _mixout_kernel(x_ref, ys_ref, yg_ref, gt_ref, mod_ref, nw_ref, wsu_ref, wgu_ref, wo_ref, o_ref):
    d = x_ref.shape[1]
    up_s = _dot(ys_ref[...], wsu_ref[...])
    up_g = _dot(yg_ref[...], wgu_ref[...])
    gates = jax.nn.sigmoid(gt_ref[...].astype(F32))
    merged = gates[:, :d] * up_s + gates[:, d:] * up_g
    m2 = _dot(merged.astype(BF16), wo_ref[...])
    o_ref[...] = x_ref[...] + mod_ref[2:3, :] * _rms_rows(m2, nw_ref[...])


def _mixout_call(x, ys, yg, gates, mod3, norm_w, w_su, w_gu, w_o, tm):
    bsz, s, d = x.shape
    blk = lambda w: pl.BlockSpec((None, tm, w), lambda b, i: (b, i, 0))
    return pl.pallas_call(
        _mixout_kernel,
        grid=(bsz, s // tm),
        in_specs=[blk(d), blk(ys.shape[-1]), blk(yg.shape[-1]), blk(gates.shape[-1]),
                  pl.BlockSpec((None, 6, d), lambda b, i: (b, 0, 0)),
                  _const_spec((1, d)), _const_spec(w_su.shape), _const_spec(w_gu.shape),
                  _const_spec(w_o.shape)],
        out_specs=blk(d),
        out_shape=jax.ShapeDtypeStruct((bsz, s, d), F32),
        compiler_params=_params(("parallel", "parallel")),
        name="mix_out",
    )(x, ys, yg, gates, mod3, norm_w.reshape(1, d), w_su, w_gu, w_o)


MLP_HIDDEN_CHUNK = 1024


def _mlp_kernel(x_ref, mod_ref, nw1_ref, nw2_ref, wu_ref, wd_ref, o_ref):
    x = x_ref[...]
    h = _rms_rows(x, nw1_ref[...]) * (1.0 + mod_ref[4:5, :]) + mod_ref[3:4, :]
    hb = h.astype(BF16)
    hidden = wu_ref.shape[1]
    y = None
    for c0 in range(0, hidden, MLP_HIDDEN_CHUNK):
        up = _dot(hb, wu_ref[:, c0:c0 + MLP_HIDDEN_CHUNK])
        act = jnp.square(jnp.maximum(up, 0.0)).astype(BF16)
        part = _dot(act, wd_ref[c0:c0 + MLP_HIDDEN_CHUNK, :])
        y = part if y is None else y + part
    o_ref[...] = x + mod_ref[5:6, :] * _rms_rows(y, nw2_ref[...])


def _mlp_call(x, mod3, nw_pre, nw_post, w_up, w_down, tm):
    bsz, s, d = x.shape
    blk = pl.BlockSpec((None, tm, d), lambda b, i: (b, i, 0))
    return pl.pallas_call(
        _mlp_kernel,
        grid=(bsz, s // tm),
        in_specs=[blk, pl.BlockSpec((None, 6, d), lambda b, i: (b, 0, 0)),
                  _const_spec((1, d)), _const_spec((1, d)),
                  _const_spec(w_up.shape), _const_spec(w_down.shape)],
        out_specs=blk,
        out_shape=jax.ShapeDtypeStruct((bsz, s, d), F32),
        compiler_params=_params(("parallel", "parallel")),
        name="mlp",
    )(x, mod3, nw_pre.reshape(1, d), nw_post.reshape(1, d), w_up, w_down)


def _row_tile(s, want):
    t = min(want, s)
    while s % t:
        t //= 2
    return t


def kernel(x, c, w_ada, b_ada, norm_mix_pre, norm_mix_post, w_in, ssm_conv_w, ssm_conv_b, ssm_dt_bias, ssm_A_log, ssm_D, ssm_norm_w, gdn_conv_w, gdn_dt_bias, gdn_A_log, gdn_norm_w, w_ssm_up, w_gdn_up, w_out, norm_mlp_pre, norm_mlp_post, w_mlp_up, w_mlp_down):
    bsz, s, d = x.shape
    depth = w_in.shape[0]
    ssm_inner = w_ssm_up.shape[1]
    ssm_heads = ssm_dt_bias.shape[1]
    ssm_conv = ssm_conv_w.shape[2]
    gdn_conv = gdn_conv_w.shape[2]
    gdn_val = w_gdn_up.shape[1]
    gdn_heads = gdn_dt_bias.shape[1]
    assert s % SSM_CHUNK == 0 and s % GDN_CHUNK == 0
    sizes = (ssm_inner, ssm_conv, ssm_heads, gdn_conv, gdn_val, gdn_heads, gdn_heads, d, d)
    offs = [0]
    for w in sizes:
        offs.append(offs[-1] + w)
    assert offs[-1] == w_in.shape[2]
    b_lane = ssm_heads
    a_lane = ssm_heads + gdn_heads
    assert a_lane + gdn_heads <= LANES

    for l in range(depth):
        mod = _mod_call(c, w_ada[l], b_ada[l])
        mod3 = mod.reshape(bsz, 6, d)
        wl = w_in[l]
        seg = lambda i: wl[:, offs[i]:offs[i + 1]]
        w_small = jnp.pad(jnp.concatenate([seg(2), seg(5), seg(6)], axis=1),
                          ((0, 0), (0, LANES - a_lane - gdn_heads)))
        seg_ws = [seg(0), seg(1), seg(3), seg(4), wl[:, offs[7]:offs[9]], w_small]
        seg_ws = [w.astype(BF16) for w in seg_ws]
        z_ssm, xbc, qkv, z_gdn, gates, small = _inproj_call(
            x, mod3, norm_mix_pre[l], seg_ws, [BF16, BF16, BF16, BF16, BF16, F32], _row_tile(s, 256))
        y_ssm = _ssd_call(z_ssm, xbc, small, ssm_conv_w[l], ssm_conv_b[l], ssm_dt_bias[l],
                          ssm_A_log[l], ssm_D[l], ssm_norm_w[l])
        y_gdn = _gdn_call(qkv, z_gdn, small, gdn_conv_w[l], gdn_dt_bias[l], gdn_A_log[l],
                          gdn_norm_w[l], b_lane, a_lane)
        x = _mixout_call(x, y_ssm, y_gdn, gates, mod3, norm_mix_post[l],
                         w_ssm_up[l].astype(BF16), w_gdn_up[l].astype(BF16), w_out[l].astype(BF16),
                         _row_tile(s, 512))
        x = _mlp_call(x, mod3, norm_mlp_pre[l], norm_mlp_post[l],
                      w_mlp_up[l].astype(BF16), w_mlp_down[l].astype(BF16), _row_tile(s, 512))
    return x
```

```python
import functools

import jax
import jax.numpy as jnp
from jax import lax
from jax.experimental import pallas as pl
from jax.experimental.pallas import tpu as pltpu

F32 = jnp.float32
BF16 = jnp.bfloat16

EPS = 1e-6
CONV_K = 4
LANES = 128
NEG_BIG = -1e30

SSM_HEAD_DIM = 64
SSM_N_GROUPS = 8
SSM_D_STATE = 128
SSM_CHUNK = 128
GDN_HEAD = 128
GDN_CHUNK = 64

VMEM_LIMIT = 56 * 1024 * 1024


def _params(sem, vmem=VMEM_LIMIT):
    return pltpu.CompilerParams(dimension_semantics=sem, vmem_limit_bytes=vmem)


def _const_spec(shape):
    nd = len(shape)
    return pl.BlockSpec(shape, lambda *_: (0,) * nd, pipeline_mode=pl.Buffered(1))


def _dot(a, b):
    return jnp.dot(a, b, preferred_element_type=F32)


def _dot_nt(a, b):
    return lax.dot_general(a, b, (((1,), (1,)), ((), ())), preferred_element_type=F32)


def _dot_tn(a, b):
    return lax.dot_general(a, b, (((0,), (0,)), ((), ())), preferred_element_type=F32)


def _split(x, n):
    parts = []
    r = x
    for i in range(n):
        p = r.astype(BF16)
        parts.append(p)
        if i + 1 < n:
            r = r - p.astype(F32)
    return parts


def _sel_left(m01, x, n=3):
    out = None
    for p in reversed(_split(x, n)):
        t = _dot(m01, p)
        out = t if out is None else out + t
    return out


def _sel_right(x, m01, n=3):
    out = None
    for p in reversed(_split(x, n)):
        t = _dot(p, m01)
        out = t if out is None else out + t
    return out


def _silu(x):
    return x * jax.nn.sigmoid(x)


def _softplus(x):
    return jnp.maximum(x, 0.0) + jnp.log1p(jnp.exp(-jnp.abs(x)))


def _rms_rows(x, w):
    return x * lax.rsqrt(jnp.mean(x * x, axis=-1, keepdims=True) + EPS) * w


def _tri_mask(n, strict=False):
    r = lax.broadcasted_iota(jnp.int32, (n, n), 0)
    c = lax.broadcasted_iota(jnp.int32, (n, n), 1)
    return (r > c) if strict else (r >= c)


def _conv_silu(hist_ref, x_ref, w_ref, bias, rows):
    hist_ref[pl.ds(8, rows), :] = x_ref[...].astype(F32)
    acc = bias
    for k in range(CONV_K):
        acc = acc + w_ref[k:k + 1, :] * hist_ref[pl.ds(8 - (CONV_K - 1) + k, rows), :]
    hist_ref[pl.ds(0, 8), :] = hist_ref[pl.ds(rows, 8), :]
    return _silu(acc)


def _mod_kernel(c_ref, w_ref, b_ref, o_ref):
    ca = _silu(c_ref[...]).astype(BF16)
    o_ref[...] = _dot(ca, w_ref[...].astype(BF16)) + b_ref[...]


def _mod_call(c, w_ada, b_ada):
    bsz, d = c.shape
    n = w_ada.shape[1]
    return pl.pallas_call(
        _mod_kernel,
        grid=(n // d,),
        in_specs=[pl.BlockSpec((bsz, d), lambda j: (0, 0)),
                  pl.BlockSpec((d, d), lambda j: (0, j)),
                  pl.BlockSpec((1, d), lambda j: (0, j))],
        out_specs=pl.BlockSpec((bsz, d), lambda j: (0, j)),
        out_shape=jax.ShapeDtypeStruct((bsz, n), F32),
        compiler_params=_params(("arbitrary",)),
        name="mod",
    )(c, w_ada, b_ada.reshape(1, n))


IN_PROJ_COL_CHUNK = 512


def _inproj_kernel(x_ref, mod_ref, nw_ref, *refs):
    n_seg = len(refs) // 2
    w_refs, o_refs = refs[:n_seg], refs[n_seg:]
    h = _rms_rows(x_ref[...], nw_ref[...]) * (1.0 + mod_ref[1:2, :]) + mod_ref[0:1, :]
    hb = h.astype(BF16)
    for w_ref, o_ref in zip(w_refs, o_refs):
        width = w_ref.shape[1]
        step = min(IN_PROJ_COL_CHUNK, width)
        for c0 in range(0, width, step):
            o_ref[:, c0:c0 + step] = _dot(hb, w_ref[:, c0:c0 + step]).astype(o_ref.dtype)


def _inproj_call(x, mod3, norm_w, seg_ws, seg_dtypes, tm):
    bsz, s, d = x.shape
    in_specs = [pl.BlockSpec((None, tm, d), lambda b, i: (b, i, 0)),
                pl.BlockSpec((None, 6, d), lambda b, i: (b, 0, 0)),
                _const_spec((1, d))]
    in_specs += [_const_spec(w.shape) for w in seg_ws]
    out_specs = [pl.BlockSpec((None, tm, w.shape[1]), lambda b, i: (b, i, 0)) for w in seg_ws]
    out_shape = [jax.ShapeDtypeStruct((bsz, s, w.shape[1]), dt) for w, dt in zip(seg_ws, seg_dtypes)]
    return pl.pallas_call(
        _inproj_kernel,
        grid=(bsz, s // tm),
        in_specs=in_specs,
        out_specs=out_specs,
        out_shape=out_shape,
        compiler_params=_params(("parallel", "parallel")),
        name="in_proj",
    )(x, mod3, norm_w.reshape(1, d), *seg_ws)


def _ssd_kernel(z_ref, xbc_ref, sm_ref, cw_ref, cb_ref, dtb_ref, alog_ref, dskip_ref, nw_ref,
                e_ref, o_ref, hist_ref, state_ref, *, d_inner):
    L = SSM_CHUNK
    G = SSM_N_GROUPS
    N = SSM_D_STATE
    gw = d_inner // G
    hg = gw // SSM_HEAD_DIM

    @pl.when(pl.program_id(1) == 0)
    def _():
        hist_ref[pl.ds(0, 8), :] = jnp.zeros((8, hist_ref.shape[1]), F32)
        state_ref[...] = jnp.zeros_like(state_ref)

    xbc = _conv_silu(hist_ref, xbc_ref, cw_ref, cb_ref[...], L)
    xs = xbc[:, :d_inner]

    dt = _softplus(sm_ref[...] + dtb_ref[...])
    a = dt * (-jnp.exp(alog_ref[...]))
    tri = _tri_mask(L).astype(BF16)
    acum = _sel_left(tri, a)
    acum_t = acum.T
    a_last = acum[L - 1:L, :]
    e = e_ref[...]
    dt_e = _sel_right(dt, e, 2)
    eac_e = _sel_right(jnp.exp(acum), e, 2)
    ws_e = _sel_right(jnp.exp(a_last - acum), e, 2)
    ea_last_e = eac_e[L - 1:L, :]

    xdt = xs * dt_e
    xdt_b = xdt.astype(BF16)
    xw_b = (xdt * ws_e).astype(BF16)
    causal = _tri_mask(L)
    lane_head = lax.broadcasted_iota(jnp.int32, (L, gw), 1) // SSM_HEAD_DIM

    for g in range(G):
        cs = slice(g * gw, (g + 1) * gw)
        b_g = xbc[:, d_inner + g * N: d_inner + (g + 1) * N].astype(BF16)
        c_g = xbc[:, d_inner + G * N + g * N: d_inner + G * N + (g + 1) * N].astype(BF16)
        cb = _dot_nt(c_g, b_g)
        xdt_g = xdt_b[:, cs]
        m_parts, x_parts = [], []
        for j in range(hg):
            h = g * hg + j
            seg = acum[:, h:h + 1] - acum_t[h:h + 1, :]
            decay = jnp.exp(jnp.where(causal, seg, NEG_BIG))
            m_parts.append((cb * decay).astype(BF16))
            x_parts.append(jnp.where(lane_head == j, xdt_g, jnp.zeros_like(xdt_g)))
        y_diag = _dot(jnp.concatenate(m_parts, axis=1), jnp.concatenate(x_parts, axis=0))
        st = state_ref[g]
        y_off = _dot(c_g, st.astype(BF16)) * eac_e[:, cs]
        state_ref[g] = st * ea_last_e[:, cs] + _dot_tn(b_g, xw_b[:, cs])
        y = y_diag + y_off + dskip_ref[:, cs] * xs[:, cs]
        y = y * _silu(z_ref[:, cs].astype(F32))
        o_ref[:, cs] = _rms_rows(y, nw_ref[:, cs]).astype(o_ref.dtype)


def _ssd_call(z, xbc, small, conv_w, conv_b, dt_bias, a_log, d_skip, norm_w):
    bsz, s, d_inner = z.shape
    conv_dim = xbc.shape[-1]
    n_heads = dt_bias.shape[0]
    L = SSM_CHUNK
    gw = d_inner // SSM_N_GROUPS
    pad = lambda v: jnp.pad(v.astype(F32), (0, LANES - n_heads)).reshape(1, LANES)
    expand = (jnp.arange(LANES)[:, None] == (jnp.arange(d_inner) // SSM_HEAD_DIM)[None, :]).astype(BF16)
    d_e = jnp.repeat(d_skip.astype(F32), SSM_HEAD_DIM).reshape(1, d_inner)
    blk = lambda w: pl.BlockSpec((None, L, w), lambda b, c: (b, c, 0))
    return pl.pallas_call(
        functools.partial(_ssd_kernel, d_inner=d_inner),
        grid=(bsz, s // L),
        in_specs=[blk(d_inner), blk(conv_dim), blk(LANES),
                  _const_spec((CONV_K, conv_dim)), _const_spec((1, conv_dim)),
                  _const_spec((1, LANES)), _const_spec((1, LANES)),
                  _const_spec((1, d_inner)), _const_spec((1, d_inner)),
                  _const_spec((LANES, d_inner))],
        out_specs=blk(d_inner),
        out_shape=jax.ShapeDtypeStruct((bsz, s, d_inner), BF16),
        scratch_shapes=[pltpu.VMEM((8 + L, conv_dim), F32),
                        pltpu.VMEM((SSM_N_GROUPS, SSM_D_STATE, gw), F32)],
        compiler_params=_params(("parallel", "arbitrary")),
        name="ssd",
    )(z, xbc, small, conv_w.astype(F32), conv_b.astype(F32).reshape(1, conv_dim),
      pad(dt_bias), pad(a_log), d_e, norm_w.astype(F32).reshape(1, d_inner), expand)


GDN_TIME_BLOCK = 128


def _unit_lower_inverses(a_list, n):
    r = lax.broadcasted_iota(jnp.int32, (n, n), 0)
    c = lax.broadcasted_iota(jnp.int32, (n, n), 1)
    eye = (r == c).astype(F32)
    invs = None
    b = 1
    while b < n:
        off = ((r // (2 * b)) == (c // (2 * b))) & ((r // b) % 2 == 1) & ((c // b) % 2 == 0)
        a_offs = [jnp.where(off, a, 0.0) for a in a_list]
        if b == 1:
            invs = [eye - a for a in a_offs]
        else:
            inv_b = [m.astype(BF16) for m in invs]
            ts = [_dot(m, a.astype(BF16)) for m, a in zip(inv_b, a_offs)]
            invs = [m - _dot(t.astype(BF16), mb) for m, t, mb in zip(invs, ts, inv_b)]
        b *= 2
    return invs


def _gdn_kernel(qkv_ref, z_ref, sm_ref, cw_ref, dtb_ref, alog_ref, nw_ref, eb_ref, eg_ref,
                o_ref, hist_ref, state_ref, *, key_dim, b_lane, a_lane):
    L = GDN_CHUNK
    dh = GDN_HEAD
    rows = qkv_ref.shape[0]
    nck = rows // L
    n_qk = key_dim // dh
    n_v = (qkv_ref.shape[1] - 2 * key_dim) // dh
    rep = n_v // n_qk

    @pl.when(pl.program_id(1) == 0)
    def _():
        hist_ref[pl.ds(0, 8), :] = jnp.zeros((8, hist_ref.shape[1]), F32)
        state_ref[...] = jnp.zeros_like(state_ref)

    qkv = _conv_silu(hist_ref, qkv_ref, cw_ref, 0.0, rows)

    sm = sm_ref[...]
    beta = jax.nn.sigmoid(sm)
    gl = -jnp.exp(alog_ref[...]) * _softplus(sm + dtb_ref[...])
    rr = lax.broadcasted_iota(jnp.int32, (rows, rows), 0)
    cc = lax.broadcasted_iota(jnp.int32, (rows, rows), 1)
    tri = ((rr >= cc) & (rr // L == cc // L)).astype(BF16)
    gcum = _sel_left(tri, gl)
    gcum_t = gcum.T
    g_last = jnp.concatenate(
        [jnp.broadcast_to(gcum[(ck + 1) * L - 1:(ck + 1) * L, :], (L, LANES)) for ck in range(nck)], axis=0)
    beta_e = _sel_right(beta, eb_ref[...], 2)
    egc_e = _sel_right(jnp.exp(gcum), eg_ref[...], 2)
    kdec_e = _sel_right(jnp.exp(g_last - gcum), eg_ref[...], 2)

    causal = _tri_mask(L)
    strict = _tri_mask(L, strict=True)
    scale = dh ** -0.5

    qs, ks, k_bs = [], [], []
    for j in range(n_qk):
        q = qkv[:, j * dh:(j + 1) * dh]
        k = qkv[:, key_dim + j * dh: key_dim + (j + 1) * dh]
        qs.append(q * (lax.rsqrt(jnp.sum(q * q, axis=-1, keepdims=True) + EPS) * scale))
        k = k * lax.rsqrt(jnp.sum(k * k, axis=-1, keepdims=True) + EPS)
        ks.append(k)
        k_bs.append(k.astype(BF16))
    q_bs = [q.astype(BF16) for q in qs]
    probs = [(ck, h) for ck in range(nck) for h in range(n_v)]
    rs = lambda ck: slice(ck * L, (ck + 1) * L)
    hs = lambda h: slice(h * dh, (h + 1) * dh)
    kk = {(ck, j): _dot_nt(k_bs[j][rs(ck)], k_bs[j][rs(ck)]) for ck in range(nck) for j in range(n_qk)}
    qk = {(ck, j): _dot_nt(q_bs[j][rs(ck)], k_bs[j][rs(ck)]) for ck in range(nck) for j in range(n_qk)}
    dmat, a_low = {}, {}
    for ck, h in probs:
        la = a_lane + h
        seg = gcum[rs(ck), la:la + 1] - gcum_t[la:la + 1, rs(ck)]
        dmat[ck, h] = jnp.exp(jnp.where(causal, seg, NEG_BIG))
        a_low[ck, h] = (jnp.where(strict, kk[ck, h // rep] * dmat[ck, h], 0.0)
                        * beta[rs(ck), b_lane + h:b_lane + h + 1])
    t_inv = dict(zip(probs, _unit_lower_inverses([a_low[p] for p in probs], L)))
    vb = qkv[:, 2 * key_dim:] * beta_e
    kbg = beta_e * egc_e
    sols, w_b, attn_b, qg_b, kd_b = {}, {}, {}, {}, {}
    for ck, h in probs:
        j = h // rep
        rhs = jnp.concatenate([vb[rs(ck), hs(h)], ks[j][rs(ck)] * kbg[rs(ck), hs(h)]], axis=1)
        sols[ck, h] = _dot(t_inv[ck, h].astype(BF16), rhs.astype(BF16))
    for ck, h in probs:
        j = h // rep
        w_b[ck, h] = sols[ck, h][:, dh:].astype(BF16)
        attn_b[ck, h] = (qk[ck, j] * dmat[ck, h]).astype(BF16)
        qg_b[ck, h] = (qs[j][rs(ck)] * egc_e[rs(ck), hs(h)]).astype(BF16)
        kd_b[ck, h] = (ks[j][rs(ck)] * kdec_e[rs(ck), hs(h)]).astype(BF16)

    heads = range(n_v)
    for ck in range(nck):
        st = [state_ref[h] for h in heads]
        st_b = [s.astype(BF16) for s in st]
        ws = [_dot(w_b[ck, h], st_b[h]) for h in heads]
        o_in = [_dot(qg_b[ck, h], st_b[h]) for h in heads]
        v_new_b = [(sols[ck, h][:, :dh] - ws[h]).astype(BF16) for h in heads]
        upd = [_dot_tn(kd_b[ck, h], v_new_b[h]) for h in heads]
        o_out = [_dot(attn_b[ck, h], v_new_b[h]) for h in heads]
        last = (ck + 1) * L - 1
        for h in heads:
            state_ref[h] = st[h] * egc_e[last:last + 1, hs(h)] + upd[h]
        for h in heads:
            o = _rms_rows(o_in[h] + o_out[h], nw_ref[...]) * _silu(z_ref[rs(ck), hs(h)].astype(F32))
            o_ref[rs(ck), hs(h)] = o.astype(o_ref.dtype)


def _gdn_call(qkv, z, small, conv_w, dt_bias, a_log, norm_w, b_lane, a_lane):
    bsz, s, conv_dim = qkv.shape
    val_dim = z.shape[-1]
    key_dim = (conv_dim - val_dim) // 2
    n_v = val_dim // GDN_HEAD
    tb = min(GDN_TIME_BLOCK, s)
    assert s % tb == 0 and tb % GDN_CHUNK == 0
    place = lambda v, at: jnp.pad(v.astype(F32), (at, LANES - at - n_v)).reshape(1, LANES)
    head_of = (jnp.arange(val_dim) // GDN_HEAD)[None, :]
    lane = jnp.arange(LANES)[:, None]
    e_b = (lane == head_of + b_lane).astype(BF16)
    e_g = (lane == head_of + a_lane).astype(BF16)
    blk = lambda w: pl.BlockSpec((None, tb, w), lambda b, c: (b, c, 0))
    return pl.pallas_call(
        functools.partial(_gdn_kernel, key_dim=key_dim, b_lane=b_lane, a_lane=a_lane),
        grid=(bsz, s // tb),
        in_specs=[blk(conv_dim), blk(val_dim), blk(LANES),
                  _const_spec((CONV_K, conv_dim)),
                  _const_spec((1, LANES)), _const_spec((1, LANES)), _const_spec((1, GDN_HEAD)),
                  _const_spec((LANES, val_dim)), _const_spec((LANES, val_dim))],
        out_specs=blk(val_dim),
        out_shape=jax.ShapeDtypeStruct((bsz, s, val_dim), BF16),
        scratch_shapes=[pltpu.VMEM((8 + tb, conv_dim), F32),
                        pltpu.VMEM((n_v, GDN_HEAD, GDN_HEAD), F32)],
        compiler_params=_params(("parallel", "arbitrary")),
        name="gdn",
    )(qkv, z, small, conv_w.astype(F32), place(dt_bias, a_lane), place(a_log, a_lane),
      norm_w.astype(F32).reshape(1, GDN_HEAD), e_b, e_g)


def _mixout_kernel(x_ref, ys_ref, yg_ref, gt_ref, mod_ref, nw_ref, wsu_ref, wgu_ref, wo_ref, o_ref):
    d = x_ref.shape[1]
    up_s = _dot(ys_ref[...], wsu_ref[...])
    up_g = _dot(yg_ref[...], wgu_ref[...])
    gates = jax.nn.sigmoid(gt_ref[...].astype(F32))
    merged = gates[:, :d] * up_s + gates[:, d:] * up_g
    m2 = _dot(merged.astype(BF16), wo_ref[...])
    o_ref[...] = x_ref[...] + mod_ref[2:3, :] * _rms_rows(m2, nw_ref[...])


def _mixout_call(x, ys, yg, gates, mod3, norm_w, w_su, w_gu, w_o, tm):
    bsz, s, d = x.shape
    blk = lambda w: pl.BlockSpec((None, tm, w), lambda b, i: (b, i, 0))
    return pl.pallas_call(
        _mixout_kernel,
        grid=(bsz, s // tm),
        in_specs=[blk(d), blk(ys.shape[-1]), blk(yg.shape[-1]), blk(gates.shape[-1]),
                  pl.BlockSpec((None, 6, d), lambda b, i: (b, 0, 0)),
                  _const_spec((1, d)), _const_spec(w_su.shape), _const_spec(w_gu.shape),
                  _const_spec(w_o.shape)],
        out_specs=blk(d),
        out_shape=jax.ShapeDtypeStruct((bsz, s, d), F32),
        compiler_params=_params(("parallel", "parallel")),
        name="mix_out",
    )(x, ys, yg, gates, mod3, norm_w.reshape(1, d), w_su, w_gu, w_o)


MLP_HIDDEN_CHUNK = 1024


def _mlp_kernel(x_ref, mod_ref, nw1_ref, nw2_ref, wu_ref, wd_ref, o_ref):
    x = x_ref[...]
    h = _rms_rows(x, nw1_ref[...]) * (1.0 + mod_ref[4:5, :]) + mod_ref[3:4, :]
    hb = h.astype(BF16)
    hidden = wu_ref.shape[1]
    y = None
    for c0 in range(0, hidden, MLP_HIDDEN_CHUNK):
        up = _dot(hb, wu_ref[:, c0:c0 + MLP_HIDDEN_CHUNK])
        act = jnp.square(jnp.maximum(up, 0.0)).astype(BF16)
        part = _dot(act, wd_ref[c0:c0 + MLP_HIDDEN_CHUNK, :])
        y = part if y is None else y + part
    o_ref[...] = x + mod_ref[5:6, :] * _rms_rows(y, nw2_ref[...])


def _mlp_call(x, mod3, nw_pre, nw_post, w_up, w_down, tm):
    bsz, s, d = x.shape
    blk = pl.BlockSpec((None, tm, d), lambda b, i: (b, i, 0))
    return pl.pallas_call(
        _mlp_kernel,
        grid=(bsz, s // tm),
        in_specs=[blk, pl.BlockSpec((None, 6, d), lambda b, i: (b, 0, 0)),
                  _const_spec((1, d)), _const_spec((1, d)),
                  _const_spec(w_up.shape), _const_spec(w_down.shape)],
        out_specs=blk,
        out_shape=jax.ShapeDtypeStruct((bsz, s, d), F32),
        compiler_params=_params(("parallel", "parallel")),
        name="mlp",
    )(x, mod3, nw_pre.reshape(1, d), nw_post.reshape(1, d), w_up, w_down)


def _row_tile(s, want):
    t = min(want, s)
    while s % t:
        t //= 2
    return t


def kernel(x, c, w_ada, b_ada, norm_mix_pre, norm_mix_post, w_in, ssm_conv_w, ssm_conv_b, ssm_dt_bias, ssm_A_log, ssm_D, ssm_norm_w, gdn_conv_w, gdn_dt_bias, gdn_A_log, gdn_norm_w, w_ssm_up, w_gdn_up, w_out, norm_mlp_pre, norm_mlp_post, w_mlp_up, w_mlp_down):
    bsz, s, d = x.shape
    depth = w_in.shape[0]
    ssm_inner = w_ssm_up.shape[1]
    ssm_heads = ssm_dt_bias.shape[1]
    ssm_conv = ssm_conv_w.shape[2]
    gdn_conv = gdn_conv_w.shape[2]
    gdn_val = w_gdn_up.shape[1]
    gdn_heads = gdn_dt_bias.shape[1]
    assert s % SSM_CHUNK == 0 and s % GDN_CHUNK == 0
    sizes = (ssm_inner, ssm_conv, ssm_heads, gdn_conv, gdn_val, gdn_heads, gdn_heads, d, d)
    offs = [0]
    for w in sizes:
        offs.append(offs[-1] + w)
    assert offs[-1] == w_in.shape[2]
    b_lane = ssm_heads
    a_lane = ssm_heads + gdn_heads
    assert a_lane + gdn_heads <= LANES

    for l in range(depth):
        mod = _mod_call(c, w_ada[l], b_ada[l])
        mod3 = mod.reshape(bsz, 6, d)
        wl = w_in[l]
        seg = lambda i: wl[:, offs[i]:offs[i + 1]]
        w_small = jnp.pad(jnp.concatenate([seg(2), seg(5), seg(6)], axis=1),
                          ((0, 0), (0, LANES - a_lane - gdn_heads)))
        seg_ws = [seg(0), seg(1), seg(3), seg(4), wl[:, offs[7]:offs[9]], w_small]
        seg_ws = [w.astype(BF16) for w in seg_ws]
        z_ssm, xbc, qkv, z_gdn, gates, small = _inproj_call(
            x, mod3, norm_mix_pre[l], seg_ws, [BF16, BF16, BF16, BF16, BF16, F32], _row_tile(s, 256))
        y_ssm = _ssd_call(z_ssm, xbc, small, ssm_conv_w[l], ssm_conv_b[l], ssm_dt_bias[l],
                          ssm_A_log[l], ssm_D[l], ssm_norm_w[l])
        y_gdn = _gdn_call(qkv, z_gdn, small, gdn_conv_w[l], gdn_dt_bias[l], gdn_A_log[l],
                          gdn_norm_w[l], b_lane, a_lane)
        x = _mixout_call(x, y_ssm, y_gdn, gates, mod3, norm_mix_post[l],
                         w_ssm_up[l].astype(BF16), w_gdn_up[l].astype(BF16), w_out[l].astype(BF16),
                         _row_tile(s, 512))
        x = _mlp_call(x, mod3, norm_mlp_pre[l], norm_mlp_post[l],
                      w_mlp_up[l].astype(BF16), w_mlp_down[l].astype(BF16), _row_tile(s, 512))
    return x
```

```python
import functools

import jax
import jax.numpy as jnp
from jax import lax
from jax.experimental import pallas as pl
from jax.experimental.pallas import tpu as pltpu

F32 = jnp.float32
BF16 = jnp.bfloat16

EPS = 1e-6
CONV_K = 4
LANES = 128
SUBLANES = 8
NEG_BIG = -1e30

SSM_HEAD_DIM = 64
SSM_N_GROUPS = 8
SSM_D_STATE = 128
SSM_CHUNK = 128
GDN_HEAD = 128
GDN_CHUNK = 64

VMEM_LIMIT = 56 * 1024 * 1024


def _params(sem, vmem=VMEM_LIMIT):
    return pltpu.CompilerParams(dimension_semantics=sem, vmem_limit_bytes=vmem)


def _const_spec(shape):
    nd = len(shape)
    return pl.BlockSpec(shape, lambda *_: (0,) * nd, pipeline_mode=pl.Buffered(1))


def _dot(a, b):
    return jnp.dot(a, b, preferred_element_type=F32)


def _dot_nt(a, b):
    return lax.dot_general(a, b, (((1,), (1,)), ((), ())), preferred_element_type=F32)


def _dot_tn(a, b):
    return lax.dot_general(a, b, (((0,), (0,)), ((), ())), preferred_element_type=F32)


def _split(x, n):
    parts = []
    r = x
    for i in range(n):
        p = r.astype(BF16)
        parts.append(p)
        if i + 1 < n:
            r = r - p.astype(F32)
    return parts


def _sel_left(m01, x, n=3):
    out = None
    for p in reversed(_split(x, n)):
        t = _dot(m01, p)
        out = t if out is None else out + t
    return out


def _sel_right(x, m01, n=3):
    out = None
    for p in reversed(_split(x, n)):
        t = _dot(p, m01)
        out = t if out is None else out + t
    return out


def _silu(x):
    h = 0.5 * x
    return h + h * jnp.tanh(h)


def _softplus(x):
    return jnp.maximum(x, 0.0) + jnp.log1p(jnp.exp(-jnp.abs(x)))


def _rms_rows(x, w):
    return x * lax.rsqrt(jnp.mean(x * x, axis=-1, keepdims=True) + EPS) * w


def _tri_mask(n, strict=False):
    r = lax.broadcasted_iota(jnp.int32, (n, n), 0)
    c = lax.broadcasted_iota(jnp.int32, (n, n), 1)
    return (r > c) if strict else (r >= c)


def _conv_silu(x, prev, w, bias):
    rows, c = x.shape
    x3 = x.reshape(rows // SUBLANES, SUBLANES, c)
    row8 = lax.broadcasted_iota(jnp.int32, (1, SUBLANES, c), 1)
    acc = w[CONV_K - 1:CONV_K, :] * x3 + bias
    for j in range(1, CONV_K):
        r = pltpu.roll(x3, j, 1)
        before = jnp.concatenate([pltpu.roll(prev, j, 0)[None], r[:-1]], axis=0)
        acc = acc + w[CONV_K - 1 - j:CONV_K - j, :] * jnp.where(row8 < j, before, r)
    return _silu(acc).reshape(rows, c)


def _mod_kernel(c_ref, w_ref, b_ref, o_ref):
    ca = _silu(c_ref[...]).astype(BF16)
    o_ref[...] = _dot(ca, w_ref[...].astype(BF16)) + b_ref[...]


def _mod_call(c, w_ada, b_ada):
    bsz, d = c.shape
    n = w_ada.shape[1]
    return pl.pallas_call(
        _mod_kernel,
        grid=(n // d,),
        in_specs=[pl.BlockSpec((bsz, d), lambda j: (0, 0)),
                  pl.BlockSpec((d, d), lambda j: (0, j)),
                  pl.BlockSpec((1, d), lambda j: (0, j))],
        out_specs=pl.BlockSpec((bsz, d), lambda j: (0, j)),
        out_shape=jax.ShapeDtypeStruct((bsz, n), F32),
        compiler_params=_params(("arbitrary",)),
        name="mod",
    )(c, w_ada, b_ada.reshape(1, n))


IN_PROJ_COL_CHUNK = 256


def _inproj_kernel(x_ref, mod_ref, nw_ref, *refs, kinds):
    n_seg = len(kinds)
    n_conv = sum(k == "conv" for k in kinds)
    w_refs = refs[:n_seg]
    cw_refs = refs[n_seg:n_seg + 2 * n_conv]
    o_refs = refs[n_seg + 2 * n_conv:2 * n_seg + 2 * n_conv]
    hist_refs = refs[2 * n_seg + 2 * n_conv:]
    rows = x_ref.shape[0]

    @pl.when(pl.program_id(1) == 0)
    def _():
        for hist_ref in hist_refs:
            hist_ref[...] = jnp.zeros_like(hist_ref)

    h = _rms_rows(x_ref[...], nw_ref[...]) * (1.0 + mod_ref[1:2, :]) + mod_ref[0:1, :]
    hb = h.astype(BF16)
    heavy, light = [], []
    ci = 0
    for i, kind in enumerate(kinds):
        width = w_refs[i].shape[1]
        step = min(IN_PROJ_COL_CHUNK, width)
        chunks = [(i, ci, slice(c0, c0 + step)) for c0 in range(0, width, step)]
        (heavy if kind == "conv" else light).extend(chunks)
        ci += kind == "conv"
    tasks = []
    for n in range(max(len(heavy), len(light))):
        tasks += heavy[n:n + 1] + light[n:n + 1]

    def finish(task, y):
        i, ci, cs = task
        if kinds[i] == "conv":
            cw_ref, cb_ref, hist_ref = cw_refs[2 * ci], cw_refs[2 * ci + 1], hist_refs[ci]
            prev = hist_ref[:, cs]
            hist_ref[:, cs] = y[rows - SUBLANES:]
            y = _conv_silu(y, prev, cw_ref[:, cs], cb_ref[:, cs])
        elif kinds[i] == "silu":
            y = _silu(y)
        o_refs[i][:, cs] = y.astype(o_refs[i].dtype)

    pending = None
    for task in tasks:
        y = _dot(hb, w_refs[task[0]][:, task[2]])
        if pending is not None:
            finish(*pending)
        pending = (task, y)
    finish(*pending)


def _inproj_call(x, mod3, norm_w, seg_ws, seg_dtypes, kinds, convs, tm):
    bsz, s, d = x.shape
    in_specs = [pl.BlockSpec((None, tm, d), lambda b, i: (b, i, 0)),
                pl.BlockSpec((None, 6, d), lambda b, i: (b, 0, 0)),
                _const_spec((1, d))]
    in_specs += [_const_spec(w.shape) for w in seg_ws]
    conv_args = []
    for cw, cb in convs:
        conv_args += [cw.astype(F32), cb.astype(F32).reshape(1, -1)]
    in_specs += [_const_spec(a.shape) for a in conv_args]
    out_specs = [pl.BlockSpec((None, tm, w.shape[1]), lambda b, i: (b, i, 0)) for w in seg_ws]
    out_shape = [jax.ShapeDtypeStruct((bsz, s, w.shape[1]), dt) for w, dt in zip(seg_ws, seg_dtypes)]
    return pl.pallas_call(
        functools.partial(_inproj_kernel, kinds=tuple(kinds)),
        grid=(bsz, s // tm),
        in_specs=in_specs,
        out_specs=out_specs,
        out_shape=out_shape,
        scratch_shapes=[pltpu.VMEM((SUBLANES, cw.shape[1]), F32) for cw, _ in convs],
        compiler_params=_params(("parallel", "arbitrary")),
        name="in_proj",
    )(x, mod3, norm_w.reshape(1, d), *seg_ws, *conv_args)


def _ssd_kernel(z_ref, xbc_ref, sm_ref, dtb_ref, alog_ref, dskip_ref, nw_ref,
                e_ref, o_ref, state_ref, *, d_inner):
    L = SSM_CHUNK
    G = SSM_N_GROUPS
    N = SSM_D_STATE
    gw = d_inner // G
    hg = gw // SSM_HEAD_DIM

    @pl.when(pl.program_id(1) == 0)
    def _():
        state_ref[...] = jnp.zeros_like(state_ref)

    xs = xbc_ref[:, :d_inner].astype(F32)

    dt = _softplus(sm_ref[...] + dtb_ref[...])
    a = dt * (-jnp.exp(alog_ref[...]))
    tri = _tri_mask(L).astype(BF16)
    acum = _sel_left(tri, a)
    acum_t = acum.T
    a_last = acum[L - 1:L, :]
    e = e_ref[...]
    dt_e = _sel_right(dt, e, 2)
    eac_e = _sel_right(jnp.exp(acum), e, 2)
    ws_e = _sel_right(jnp.exp(a_last - acum), e, 2)
    ea_last_e = eac_e[L - 1:L, :]

    xdt = xs * dt_e
    xdt_b = xdt.astype(BF16)
    xw_b = (xdt * ws_e).astype(BF16)
    causal = _tri_mask(L)
    lane_head = lax.broadcasted_iota(jnp.int32, (L, gw), 1) // SSM_HEAD_DIM

    for g in range(G):
        cs = slice(g * gw, (g + 1) * gw)
        b_g = xbc_ref[:, d_inner + g * N: d_inner + (g + 1) * N]
        c_g = xbc_ref[:, d_inner + G * N + g * N: d_inner + G * N + (g + 1) * N]
        cb = _dot_nt(c_g, b_g)
        xdt_g = xdt_b[:, cs]
        m_parts, x_parts = [], []
        for j in range(hg):
            h = g * hg + j
            seg = acum[:, h:h + 1] - acum_t[h:h + 1, :]
            decay = jnp.exp(jnp.where(causal, seg, NEG_BIG))
            m_parts.append((cb * decay).astype(BF16))
            x_parts.append(jnp.where(lane_head == j, xdt_g, jnp.zeros_like(xdt_g)))
        y_diag = _dot(jnp.concatenate(m_parts, axis=1), jnp.concatenate(x_parts, axis=0))
        st = state_ref[g]
        y_off = _dot(c_g, st.astype(BF16)) * eac_e[:, cs]
        state_ref[g] = st * ea_last_e[:, cs] + _dot_tn(b_g, xw_b[:, cs])
        y = y_diag + y_off + dskip_ref[:, cs] * xs[:, cs]
        y = y * z_ref[:, cs].astype(F32)
        o_ref[:, cs] = _rms_rows(y, nw_ref[:, cs]).astype(o_ref.dtype)


def _ssd_call(z, xbc, small, dt_bias, a_log, d_skip, norm_w):
    bsz, s, d_inner = z.shape
    conv_dim = xbc.shape[-1]
    n_heads = dt_bias.shape[0]
    L = SSM_CHUNK
    gw = d_inner // SSM_N_GROUPS
    pad = lambda v: jnp.pad(v.astype(F32), (0, LANES - n_heads)).reshape(1, LANES)
    expand = (jnp.arange(LANES)[:, None] == (jnp.arange(d_inner) // SSM_HEAD_DIM)[None, :]).astype(BF16)
    d_e = jnp.repeat(d_skip.astype(F32), SSM_HEAD_DIM).reshape(1, d_inner)
    blk = lambda w: pl.BlockSpec((None, L, w), lambda b, c: (b, c, 0))
    return pl.pallas_call(
        functools.partial(_ssd_kernel, d_inner=d_inner),
        grid=(bsz, s // L),
        in_specs=[blk(d_inner), blk(conv_dim), blk(LANES),
                  _const_spec((1, LANES)), _const_spec((1, LANES)),
                  _const_spec((1, d_inner)), _const_spec((1, d_inner)),
                  _const_spec((LANES, d_inner))],
        out_specs=blk(d_inner),
        out_shape=jax.ShapeDtypeStruct((bsz, s, d_inner), BF16),
        scratch_shapes=[pltpu.VMEM((SSM_N_GROUPS, SSM_D_STATE, gw), F32)],
        compiler_params=_params(("parallel", "arbitrary")),
        name="ssd",
    )(z, xbc, small, pad(dt_bias), pad(a_log), d_e, norm_w.astype(F32).reshape(1, d_inner), expand)


GDN_TIME_BLOCK = 256


def _unit_lower_inverses(a_list, n):
    r = lax.broadcasted_iota(jnp.int32, (n, n), 0)
    c = lax.broadcasted_iota(jnp.int32, (n, n), 1)
    eye = (r == c).astype(F32)
    invs = None
    b = 1
    while b < n:
        off = ((r // (2 * b)) == (c // (2 * b))) & ((r // b) % 2 == 1) & ((c // b) % 2 == 0)
        a_offs = [jnp.where(off, a, 0.0) for a in a_list]
        if b == 1:
            invs = [eye - a for a in a_offs]
        else:
            inv_b = [m.astype(BF16) for m in invs]
            ts = [_dot(m, a.astype(BF16)) for m, a in zip(inv_b, a_offs)]
            invs = [m - _dot(t.astype(BF16), mb) for m, t, mb in zip(invs, ts, inv_b)]
        b *= 2
    return invs


def _gdn_kernel(qkv_ref, z_ref, sm_ref, dtb_ref, alog_ref, nw_ref, eb_ref, eg_ref,
                o_ref, state_ref, *, key_dim, b_lane, a_lane):
    L = GDN_CHUNK
    dh = GDN_HEAD
    rows = qkv_ref.shape[0]
    nck = rows // L
    n_qk = key_dim // dh
    n_v = (qkv_ref.shape[1] - 2 * key_dim) // dh
    rep = n_v // n_qk

    @pl.when(pl.program_id(1) == 0)
    def _():
        state_ref[...] = jnp.zeros_like(state_ref)

    qkv = qkv_ref[...].astype(F32)

    sm = sm_ref[...]
    beta = jax.nn.sigmoid(sm)
    gl = -jnp.exp(alog_ref[...]) * _softplus(sm + dtb_ref[...])
    rr = lax.broadcasted_iota(jnp.int32, (rows, rows), 0)
    cc = lax.broadcasted_iota(jnp.int32, (rows, rows), 1)
    tri = ((rr >= cc) & (rr // L == cc // L)).astype(BF16)
    gcum = _sel_left(tri, gl)
    gcum_t = gcum.T
    g_last = jnp.concatenate(
        [jnp.broadcast_to(gcum[(ck + 1) * L - 1:(ck + 1) * L, :], (L, LANES)) for ck in range(nck)], axis=0)
    beta_e = _sel_right(beta, eb_ref[...], 2)
    egc_e = _sel_right(jnp.exp(gcum), eg_ref[...], 2)
    kdec_e = _sel_right(jnp.exp(g_last - gcum), eg_ref[...], 2)

    causal = _tri_mask(L)
    strict = _tri_mask(L, strict=True)
    scale = dh ** -0.5

    qs, ks, k_bs = [], [], []
    for j in range(n_qk):
        q = qkv[:, j * dh:(j + 1) * dh]
        k = qkv[:, key_dim + j * dh: key_dim + (j + 1) * dh]
        qs.append(q * (lax.rsqrt(jnp.sum(q * q, axis=-1, keepdims=True) + EPS) * scale))
        k = k * lax.rsqrt(jnp.sum(k * k, axis=-1, keepdims=True) + EPS)
        ks.append(k)
        k_bs.append(k.astype(BF16))
    q_bs = [q.astype(BF16) for q in qs]
    probs = [(ck, h) for ck in range(nck) for h in range(n_v)]
    rs = lambda ck: slice(ck * L, (ck + 1) * L)
    hs = lambda h: slice(h * dh, (h + 1) * dh)
    kk = {(ck, j): _dot_nt(k_bs[j][rs(ck)], k_bs[j][rs(ck)]) for ck in range(nck) for j in range(n_qk)}
    qk = {(ck, j): _dot_nt(q_bs[j][rs(ck)], k_bs[j][rs(ck)]) for ck in range(nck) for j in range(n_qk)}
    dmat, a_low = {}, {}
    for ck, h in probs:
        la = a_lane + h
        seg = gcum[rs(ck), la:la + 1] - gcum_t[la:la + 1, rs(ck)]
        dmat[ck, h] = jnp.exp(jnp.where(causal, seg, NEG_BIG))
        a_low[ck, h] = (jnp.where(strict, kk[ck, h // rep] * dmat[ck, h], 0.0)
                        * beta[rs(ck), b_lane + h:b_lane + h + 1])
    t_inv = dict(zip(probs, _unit_lower_inverses([a_low[p] for p in probs], L)))
    vb = qkv[:, 2 * key_dim:] * beta_e
    kbg = beta_e * egc_e
    sols, w_b, attn_b, qg_b, kd_b = {}, {}, {}, {}, {}
    for ck, h in probs:
        j = h // rep
        rhs = jnp.concatenate([vb[rs(ck), hs(h)], ks[j][rs(ck)] * kbg[rs(ck), hs(h)]], axis=1)
        sols[ck, h] = _dot(t_inv[ck, h].astype(BF16), rhs.astype(BF16))
    for ck, h in probs:
        j = h // rep
        w_b[ck, h] = sols[ck, h][:, dh:].astype(BF16)
        attn_b[ck, h] = (qk[ck, j] * dmat[ck, h]).astype(BF16)
        qg_b[ck, h] = (qs[j][rs(ck)] * egc_e[rs(ck), hs(h)]).astype(BF16)
        kd_b[ck, h] = (ks[j][rs(ck)] * kdec_e[rs(ck), hs(h)]).astype(BF16)

    heads = range(n_v)
    for ck in range(nck):
        st = [state_ref[h] for h in heads]
        st_b = [s.astype(BF16) for s in st]
        ws = [_dot(w_b[ck, h], st_b[h]) for h in heads]
        o_in = [_dot(qg_b[ck, h], st_b[h]) for h in heads]
        v_new_b = [(sols[ck, h][:, :dh] - ws[h]).astype(BF16) for h in heads]
        upd = [_dot_tn(kd_b[ck, h], v_new_b[h]) for h in heads]
        o_out = [_dot(attn_b[ck, h], v_new_b[h]) for h in heads]
        last = (ck + 1) * L - 1
        for h in heads:
            state_ref[h] = st[h] * egc_e[last:last + 1, hs(h)] + upd[h]
        for h in heads:
            o = _rms_rows(o_in[h] + o_out[h], nw_ref[...]) * z_ref[rs(ck), hs(h)].astype(F32)
            o_ref[rs(ck), hs(h)] = o.astype(o_ref.dtype)


def _gdn_call(qkv, z, small, dt_bias, a_log, norm_w, b_lane, a_lane):
    bsz, s, conv_dim = qkv.shape
    val_dim = z.shape[-1]
    key_dim = (conv_dim - val_dim) // 2
    n_v = val_dim // GDN_HEAD
    tb = min(GDN_TIME_BLOCK, s)
    assert s % tb == 0 and tb % GDN_CHUNK == 0
    place = lambda v, at: jnp.pad(v.astype(F32), (at, LANES - at - n_v)).reshape(1, LANES)
    head_of = (jnp.arange(val_dim) // GDN_HEAD)[None, :]
    lane = jnp.arange(LANES)[:, None]
    e_b = (lane == head_of + b_lane).astype(BF16)
    e_g = (lane == head_of + a_lane).astype(BF16)
    blk = lambda w: pl.BlockSpec((None, tb, w), lambda b, c: (b, c, 0))
    return pl.pallas_call(
        functools.partial(_gdn_kernel, key_dim=key_dim, b_lane=b_lane, a_lane=a_lane),
        grid=(bsz, s // tb),
        in_specs=[blk(conv_dim), blk(val_dim), blk(LANES),
                  _const_spec((1, LANES)), _const_spec((1, LANES)), _const_spec((1, GDN_HEAD)),
                  _const_spec((LANES, val_dim)), _const_spec((LANES, val_dim))],
        out_specs=blk(val_dim),
        out_shape=jax.ShapeDtypeStruct((bsz, s, val_dim), BF16),
        scratch_shapes=[pltpu.VMEM((n_v, GDN_HEAD, GDN_HEAD), F32)],
        compiler_params=_params(("parallel", "arbitrary")),
        name="gdn",
    )(qkv, z, small, place(dt_bias, a_lane), place(a_log, a_lane),
      norm_w.astype(F32).reshape(1, GDN_HEAD), e_b, e_g)


def _mixout_kernel(x_ref, ys_ref, yg_ref, gt_ref, mod_ref, nw_ref, wsu_ref, wgu_ref, wo_ref, o_ref):
    d = x_ref.shape[1]
    up_s = _dot(ys_ref[...], wsu_ref[...])
    up_g = _dot(yg_ref[...], wgu_ref[...])
    gates = jax.nn.sigmoid(gt_ref[...].astype(F32))
    merged = gates[:, :d] * up_s + gates[:, d:] * up_g
    m2 = _dot(merged.astype(BF16), wo_ref[...])
    o_ref[...] = x_ref[...] + mod_ref[2:3, :] * _rms_rows(m2, nw_ref[...])


def _mixout_call(x, ys, yg, gates, mod3, norm_w, w_su, w_gu, w_o, tm):
    bsz, s, d = x.shape
    blk = lambda w: pl.BlockSpec((None, tm, w), lambda b, i: (b, i, 0))
    return pl.pallas_call(
        _mixout_kernel,
        grid=(bsz, s // tm),
        in_specs=[blk(d), blk(ys.shape[-1]), blk(yg.shape[-1]), blk(gates.shape[-1]),
                  pl.BlockSpec((None, 6, d), lambda b, i: (b, 0, 0)),
                  _const_spec((1, d)), _const_spec(w_su.shape), _const_spec(w_gu.shape),
                  _const_spec(w_o.shape)],
        out_specs=blk(d),
        out_shape=jax.ShapeDtypeStruct((bsz, s, d), F32),
        compiler_params=_params(("parallel", "parallel")),
        name="mix_out",
    )(x, ys, yg, gates, mod3, norm_w.reshape(1, d), w_su, w_gu, w_o)


MLP_HIDDEN_CHUNK = 1024


def _mlp_kernel(x_ref, mod_ref, nw1_ref, nw2_ref, wu_ref, wd_ref, o_ref):
    x = x_ref[...]
    h = _rms_rows(x, nw1_ref[...]) * (1.0 + mod_ref[4:5, :]) + mod_ref[3:4, :]
    hb = h.astype(BF16)
    hidden = wu_ref.shape[1]
    y = None
    for c0 in range(0, hidden, MLP_HIDDEN_CHUNK):
        up = _dot(hb, wu_ref[:, c0:c0 + MLP_HIDDEN_CHUNK])
        act = jnp.square(jnp.maximum(up, 0.0)).astype(BF16)
        part = _dot(act, wd_ref[c0:c0 + MLP_HIDDEN_CHUNK, :])
        y = part if y is None else y + part
    o_ref[...] = x + mod_ref[5:6, :] * _rms_rows(y, nw2_ref[...])


def _mlp_call(x, mod3, nw_pre, nw_post, w_up, w_down, tm):
    bsz, s, d = x.shape
    blk = pl.BlockSpec((None, tm, d), lambda b, i: (b, i, 0))
    return pl.pallas_call(
        _mlp_kernel,
        grid=(bsz, s // tm),
        in_specs=[blk, pl.BlockSpec((None, 6, d), lambda b, i: (b, 0, 0)),
                  _const_spec((1, d)), _const_spec((1, d)),
                  _const_spec(w_up.shape), _const_spec(w_down.shape)],
        out_specs=blk,
        out_shape=jax.ShapeDtypeStruct((bsz, s, d), F32),
        compiler_params=_params(("parallel", "parallel")),
        name="mlp",
    )(x, mod3, nw_pre.reshape(1, d), nw_post.reshape(1, d), w_up, w_down)


def _row_tile(s, want):
    t = min(want, s)
    while s % t:
        t //= 2
    return t


def kernel(x, c, w_ada, b_ada, norm_mix_pre, norm_mix_post, w_in, ssm_conv_w, ssm_conv_b, ssm_dt_bias, ssm_A_log, ssm_D, ssm_norm_w, gdn_conv_w, gdn_dt_bias, gdn_A_log, gdn_norm_w, w_ssm_up, w_gdn_up, w_out, norm_mlp_pre, norm_mlp_post, w_mlp_up, w_mlp_down):
    bsz, s, d = x.shape
    depth = w_in.shape[0]
    ssm_inner = w_ssm_up.shape[1]
    ssm_heads = ssm_dt_bias.shape[1]
    ssm_conv = ssm_conv_w.shape[2]
    gdn_conv = gdn_conv_w.shape[2]
    gdn_val = w_gdn_up.shape[1]
    gdn_heads = gdn_dt_bias.shape[1]
    assert s % SSM_CHUNK == 0 and s % GDN_CHUNK == 0
    sizes = (ssm_inner, ssm_conv, ssm_heads, gdn_conv, gdn_val, gdn_heads, gdn_heads, d, d)
    offs = [0]
    for w in sizes:
        offs.append(offs[-1] + w)
    assert offs[-1] == w_in.shape[2]
    b_lane = ssm_heads
    a_lane = ssm_heads + gdn_heads
    assert a_lane + gdn_heads <= LANES

    for l in range(depth):
        mod = _mod_call(c, w_ada[l], b_ada[l])
        mod3 = mod.reshape(bsz, 6, d)
        wl = w_in[l]
        seg = lambda i: wl[:, offs[i]:offs[i + 1]]
        w_small = jnp.pad(jnp.concatenate([seg(2), seg(5), seg(6)], axis=1),
                          ((0, 0), (0, LANES - a_lane - gdn_heads)))
        seg_ws = [seg(0), seg(1), seg(3), seg(4), wl[:, offs[7]:offs[9]], w_small]
        seg_ws = [w.astype(BF16) for w in seg_ws]
        kinds = ["silu", "conv", "conv", "silu", "plain", "plain"]
        convs = [(ssm_conv_w[l], ssm_conv_b[l]), (gdn_conv_w[l], jnp.zeros((gdn_conv,), F32))]
        z_ssm, xbc, qkv, z_gdn, gates, small = _inproj_call(
            x, mod3, norm_mix_pre[l], seg_ws, [BF16, BF16, BF16, BF16, BF16, F32], kinds, convs,
            _row_tile(s, 256))
        y_ssm = _ssd_call(z_ssm, xbc, small, ssm_dt_bias[l], ssm_A_log[l], ssm_D[l], ssm_norm_w[l])
        y_gdn = _gdn_call(qkv, z_gdn, small, gdn_dt_bias[l], gdn_A_log[l], gdn_norm_w[l],
                          b_lane, a_lane)
        x = _mixout_call(x, y_ssm, y_gdn, gates, mod3, norm_mix_post[l],
                         w_ssm_up[l].astype(BF16), w_gdn_up[l].astype(BF16), w_out[l].astype(BF16),
                         _row_tile(s, 512))
        x = _mlp_call(x, mod3, norm_mlp_pre[l], norm_mlp_post[l],
                      w_mlp_up[l].astype(BF16), w_mlp_down[l].astype(BF16), _row_tile(s, 512))
    return x
```

```python
import functools

import jax
import jax.numpy as jnp
from jax import lax
from jax.experimental import pallas as pl
from jax.experimental.pallas import tpu as pltpu

F32 = jnp.float32
BF16 = jnp.bfloat16

EPS = 1e-6
CONV_K = 4
LANES = 128
SUBLANES = 8
NEG_BIG = -1e30

SSM_HEAD_DIM = 64
SSM_N_GROUPS = 8
SSM_D_STATE = 128
SSM_CHUNK = 128
GDN_HEAD = 128
GDN_CHUNK = 64

VMEM_LIMIT = 56 * 1024 * 1024


def _params(sem, vmem=VMEM_LIMIT):
    return pltpu.CompilerParams(dimension_semantics=sem, vmem_limit_bytes=vmem)


def _const_spec(shape):
    nd = len(shape)
    return pl.BlockSpec(shape, lambda *_: (0,) * nd, pipeline_mode=pl.Buffered(1))


def _dot(a, b):
    return jnp.dot(a, b, preferred_element_type=F32)


def _dot_nt(a, b):
    return lax.dot_general(a, b, (((1,), (1,)), ((), ())), preferred_element_type=F32)


def _dot_tn(a, b):
    return lax.dot_general(a, b, (((0,), (0,)), ((), ())), preferred_element_type=F32)


def _split(x, n):
    parts = []
    r = x
    for i in range(n):
        p = r.astype(BF16)
        parts.append(p)
        if i + 1 < n:
            r = r - p.astype(F32)
    return parts


def _sel_left(m01, x, n=3):
    out = None
    for p in reversed(_split(x, n)):
        t = _dot(m01, p)
        out = t if out is None else out + t
    return out


def _sel_right(x, m01, n=3):
    out = None
    for p in reversed(_split(x, n)):
        t = _dot(p, m01)
        out = t if out is None else out + t
    return out


def _silu(x):
    h = 0.5 * x
    return h + h * jnp.tanh(h)


def _softplus(x):
    return jnp.maximum(x, 0.0) + jnp.log1p(jnp.exp(-jnp.abs(x)))


def _rms_rows(x, w):
    return x * lax.rsqrt(jnp.mean(x * x, axis=-1, keepdims=True) + EPS) * w


def _tri_mask(n, strict=False):
    r = lax.broadcasted_iota(jnp.int32, (n, n), 0)
    c = lax.broadcasted_iota(jnp.int32, (n, n), 1)
    return (r > c) if strict else (r >= c)


def _conv_silu(x, prev, w, bias):
    rows, c = x.shape
    x3 = x.reshape(rows // SUBLANES, SUBLANES, c)
    row8 = lax.broadcasted_iota(jnp.int32, (1, SUBLANES, c), 1)
    acc = w[CONV_K - 1:CONV_K, :] * x3 + bias
    for j in range(1, CONV_K):
        r = pltpu.roll(x3, j, 1)
        before = jnp.concatenate([pltpu.roll(prev, j, 0)[None], r[:-1]], axis=0)
        acc = acc + w[CONV_K - 1 - j:CONV_K - j, :] * jnp.where(row8 < j, before, r)
    return _silu(acc).reshape(rows, c)


def _mod_kernel(c_ref, w_ref, b_ref, o_ref):
    ca = _silu(c_ref[...]).astype(BF16)
    o_ref[...] = _dot(ca, w_ref[...].astype(BF16)) + b_ref[...]


def _mod_call(c, w_ada, b_ada):
    bsz, d = c.shape
    n = w_ada.shape[1]
    return pl.pallas_call(
        _mod_kernel,
        grid=(n // d,),
        in_specs=[pl.BlockSpec((bsz, d), lambda j: (0, 0)),
                  pl.BlockSpec((d, d), lambda j: (0, j)),
                  pl.BlockSpec((1, d), lambda j: (0, j))],
        out_specs=pl.BlockSpec((bsz, d), lambda j: (0, j)),
        out_shape=jax.ShapeDtypeStruct((bsz, n), F32),
        compiler_params=_params(("arbitrary",)),
        name="mod",
    )(c, w_ada, b_ada.reshape(1, n))


IN_PROJ_COL_CHUNK = 256


def _inproj_kernel(x_ref, mod_ref, nw_ref, *refs, kinds):
    n_seg = len(kinds)
    n_conv = sum(k == "conv" for k in kinds)
    w_refs = refs[:n_seg]
    cw_refs = refs[n_seg:n_seg + 2 * n_conv]
    o_refs = refs[n_seg + 2 * n_conv:2 * n_seg + 2 * n_conv]
    hist_refs = refs[2 * n_seg + 2 * n_conv:]
    rows = x_ref.shape[0]

    @pl.when(pl.program_id(1) == 0)
    def _():
        for hist_ref in hist_refs:
            hist_ref[...] = jnp.zeros_like(hist_ref)

    h = _rms_rows(x_ref[...], nw_ref[...]) * (1.0 + mod_ref[1:2, :]) + mod_ref[0:1, :]
    hb = h.astype(BF16)
    heavy, light = [], []
    ci = 0
    for i, kind in enumerate(kinds):
        width = w_refs[i].shape[1]
        step = min(IN_PROJ_COL_CHUNK, width)
        chunks = [(i, ci, slice(c0, c0 + step)) for c0 in range(0, width, step)]
        (heavy if kind == "conv" else light).extend(chunks)
        ci += kind == "conv"
    tasks = []
    for n in range(max(len(heavy), len(light))):
        tasks += heavy[n:n + 1] + light[n:n + 1]

    def finish(task, y):
        i, ci, cs = task
        if kinds[i] == "conv":
            cw_ref, cb_ref, hist_ref = cw_refs[2 * ci], cw_refs[2 * ci + 1], hist_refs[ci]
            prev = hist_ref[:, cs]
            hist_ref[:, cs] = y[rows - SUBLANES:]
            y = _conv_silu(y, prev, cw_ref[:, cs], cb_ref[:, cs])
        elif kinds[i] == "silu":
            y = _silu(y)
        o_refs[i][:, cs] = y.astype(o_refs[i].dtype)

    pending = None
    for task in tasks:
        y = _dot(hb, w_refs[task[0]][:, task[2]])
        if pending is not None:
            finish(*pending)
        pending = (task, y)
    finish(*pending)


def _inproj_call(x, mod3, norm_w, seg_ws, seg_dtypes, kinds, convs, tm):
    bsz, s, d = x.shape
    in_specs = [pl.BlockSpec((None, tm, d), lambda b, i: (b, i, 0)),
                pl.BlockSpec((None, 6, d), lambda b, i: (b, 0, 0)),
                _const_spec((1, d))]
    in_specs += [_const_spec(w.shape) for w in seg_ws]
    conv_args = []
    for cw, cb in convs:
        conv_args += [cw.astype(F32), cb.astype(F32).reshape(1, -1)]
    in_specs += [_const_spec(a.shape) for a in conv_args]
    out_specs = [pl.BlockSpec((None, tm, w.shape[1]), lambda b, i: (b, i, 0)) for w in seg_ws]
    out_shape = [jax.ShapeDtypeStruct((bsz, s, w.shape[1]), dt) for w, dt in zip(seg_ws, seg_dtypes)]
    return pl.pallas_call(
        functools.partial(_inproj_kernel, kinds=tuple(kinds)),
        grid=(bsz, s // tm),
        in_specs=in_specs,
        out_specs=out_specs,
        out_shape=out_shape,
        scratch_shapes=[pltpu.VMEM((SUBLANES, cw.shape[1]), F32) for cw, _ in convs],
        compiler_params=_params(("parallel", "arbitrary")),
        name="in_proj",
    )(x, mod3, norm_w.reshape(1, d), *seg_ws, *conv_args)


def _ssd_kernel(z_ref, xbc_ref, sm_ref, dtb_ref, alog_ref, dskip_ref, nw_ref,
                e_ref, o_ref, state_ref, *, d_inner):
    L = SSM_CHUNK
    G = SSM_N_GROUPS
    N = SSM_D_STATE
    gw = d_inner // G
    hg = gw // SSM_HEAD_DIM

    @pl.when(pl.program_id(1) == 0)
    def _():
        state_ref[...] = jnp.zeros_like(state_ref)

    xs = xbc_ref[:, :d_inner].astype(F32)

    dt = _softplus(sm_ref[...] + dtb_ref[...])
    a = dt * (-jnp.exp(alog_ref[...]))
    tri = _tri_mask(L).astype(BF16)
    acum = _sel_left(tri, a)
    acum_t = acum.T
    a_last = acum[L - 1:L, :]
    e = e_ref[...]
    dt_e = _sel_right(dt, e, 2)
    eac_e = _sel_right(jnp.exp(acum), e, 2)
    ws_e = _sel_right(jnp.exp(a_last - acum), e, 2)
    ea_last_e = eac_e[L - 1:L, :]

    xdt = xs * dt_e
    xdt_b = xdt.astype(BF16)
    xw_b = (xdt * ws_e).astype(BF16)
    causal = _tri_mask(L)
    lane_head = lax.broadcasted_iota(jnp.int32, (L, gw), 1) // SSM_HEAD_DIM

    for g in range(G):
        cs = slice(g * gw, (g + 1) * gw)
        b_g = xbc_ref[:, d_inner + g * N: d_inner + (g + 1) * N]
        c_g = xbc_ref[:, d_inner + G * N + g * N: d_inner + G * N + (g + 1) * N]
        cb = _dot_nt(c_g, b_g)
        xdt_g = xdt_b[:, cs]
        m_parts, x_parts = [], []
        for j in range(hg):
            h = g * hg + j
            seg = acum[:, h:h + 1] - acum_t[h:h + 1, :]
            decay = jnp.exp(jnp.where(causal, seg, NEG_BIG))
            m_parts.append((cb * decay).astype(BF16))
            x_parts.append(jnp.where(lane_head == j, xdt_g, jnp.zeros_like(xdt_g)))
        y_diag = _dot(jnp.concatenate(m_parts, axis=1), jnp.concatenate(x_parts, axis=0))
        st = state_ref[g]
        y_off = _dot(c_g, st.astype(BF16)) * eac_e[:, cs]
        state_ref[g] = st * ea_last_e[:, cs] + _dot_tn(b_g, xw_b[:, cs])
        y = y_diag + y_off + dskip_ref[:, cs] * xs[:, cs]
        y = y * z_ref[:, cs].astype(F32)
        o_ref[:, cs] = _rms_rows(y, nw_ref[:, cs]).astype(o_ref.dtype)


def _ssd_call(z, xbc, small, dt_bias, a_log, d_skip, norm_w):
    bsz, s, d_inner = z.shape
    conv_dim = xbc.shape[-1]
    n_heads = dt_bias.shape[0]
    L = SSM_CHUNK
    gw = d_inner // SSM_N_GROUPS
    pad = lambda v: jnp.pad(v.astype(F32), (0, LANES - n_heads)).reshape(1, LANES)
    expand = (jnp.arange(LANES)[:, None] == (jnp.arange(d_inner) // SSM_HEAD_DIM)[None, :]).astype(BF16)
    d_e = jnp.repeat(d_skip.astype(F32), SSM_HEAD_DIM).reshape(1, d_inner)
    blk = lambda w: pl.BlockSpec((None, L, w), lambda b, c: (b, c, 0))
    return pl.pallas_call(
        functools.partial(_ssd_kernel, d_inner=d_inner),
        grid=(bsz, s // L),
        in_specs=[blk(d_inner), blk(conv_dim), blk(LANES),
                  _const_spec((1, LANES)), _const_spec((1, LANES)),
                  _const_spec((1, d_inner)), _const_spec((1, d_inner)),
                  _const_spec((LANES, d_inner))],
        out_specs=blk(d_inner),
        out_shape=jax.ShapeDtypeStruct((bsz, s, d_inner), BF16),
        scratch_shapes=[pltpu.VMEM((SSM_N_GROUPS, SSM_D_STATE, gw), F32)],
        compiler_params=_params(("parallel", "arbitrary")),
        name="ssd",
    )(z, xbc, small, pad(dt_bias), pad(a_log), d_e, norm_w.astype(F32).reshape(1, d_inner), expand)


GDN_TIME_BLOCK = 256


def _zero_after(x):
    bits = pltpu.bitcast(x[:SUBLANES, :], jnp.uint32)
    bits = lax.shift_right_logical(lax.shift_right_logical(bits, jnp.uint32(16)), jnp.uint32(16))
    return pltpu.bitcast(bits, F32)[:1, :]


def _split_halves(x, lo):
    z = jnp.zeros_like(x)
    return jnp.concatenate([jnp.where(lo, x, z), jnp.where(lo, z, x)], axis=0)


def _unit_lower_inverse_pairs(a_list, n, fillers=()):
    fillers = list(fillers)
    r = lax.broadcasted_iota(jnp.int32, (n, 2 * n), 0)
    lane = lax.broadcasted_iota(jnp.int32, (n, 2 * n), 1)
    c = lane % n
    lo = lane < n
    eye = (r == c).astype(F32)
    invs = None
    b = 1
    while b < n:
        off = ((r // (2 * b)) == (c // (2 * b))) & ((r // b) % 2 == 1) & ((c // b) % 2 == 0)
        a_offs = [jnp.where(off, a, 0.0) for a in a_list]
        if b == 1:
            invs = [eye - a for a in a_offs]
        else:
            inv_b = [m.astype(BF16) for m in invs]
            ts = [_dot(mb, _split_halves(a.astype(BF16), lo)) for mb, a in zip(inv_b, a_offs)]
            if fillers:
                fillers.pop(0)(ts[-1])
            invs = [m - _dot(t.astype(BF16), _split_halves(mb, lo)) for m, t, mb in zip(invs, ts, inv_b)]
            if fillers:
                fillers.pop(0)(invs[-1])
        b *= 2
    for f in fillers:
        f(invs[-1])
    return invs


def _gdn_kernel(qkv_ref, z_ref, sm_ref, cw_ref, dtb_ref, alog_ref, nw_ref, eb_ref, eg_ref,
                o_ref, hist_ref, state_ref, *, key_dim, b_lane, a_lane):
    L = GDN_CHUNK
    dh = GDN_HEAD
    rows = qkv_ref.shape[0]
    nck = rows // L
    n_qk = key_dim // dh
    n_v = (qkv_ref.shape[1] - 2 * key_dim) // dh
    assert n_v == 2 * n_qk and dh == 2 * L

    @pl.when(pl.program_id(1) == 0)
    def _():
        hist_ref[...] = jnp.zeros_like(hist_ref)
        state_ref[...] = jnp.zeros_like(state_ref)

    def conv_cols(c0, after=None):
        cs = slice(c0, c0 + dh)
        raw = qkv_ref[:, cs].astype(F32)
        prev = hist_ref[:, cs]
        hist_ref[:, cs] = raw[rows - SUBLANES:]
        w = cw_ref[:, cs]
        if after is not None:
            w = w + _zero_after(after)
        return _conv_silu(raw, prev, w, 0.0)

    r = lax.broadcasted_iota(jnp.int32, (L, dh), 0)
    lane = lax.broadcasted_iota(jnp.int32, (L, dh), 1)
    c = lane % L
    lo = lane < L
    eye2, causal2, strict2 = r == c, r >= c, r > c
    lo_kt = lax.broadcasted_iota(jnp.int32, (dh, dh), 1) < L
    scale = dh ** -0.5
    rs = lambda ck: slice(ck * L, (ck + 1) * L)
    pairs = [(ck, j) for ck in range(nck) for j in range(n_qk)]

    ks, k_bs = [], []
    for j in range(n_qk):
        k = conv_cols(key_dim + j * dh)
        k = k * lax.rsqrt(jnp.sum(k * k, axis=-1, keepdims=True) + EPS)
        ks.append(k)
        k_bs.append(k.astype(BF16))
    kdup = {(ck, j): jnp.concatenate([k_bs[j][rs(ck)]] * 2, axis=0) for ck, j in pairs}
    kk2 = {p: _dot_nt(k_bs[p[1]][rs(p[0])], kdup[p]) for p in pairs}

    sm = sm_ref[...]
    beta = jax.nn.sigmoid(sm)
    gl = -jnp.exp(alog_ref[...]) * _softplus(sm + dtb_ref[...])
    rr = lax.broadcasted_iota(jnp.int32, (rows, rows), 0)
    cc = lax.broadcasted_iota(jnp.int32, (rows, rows), 1)
    tri = ((rr >= cc) & (rr // L == cc // L)).astype(BF16)
    gcum = _sel_left(tri, gl)
    cg = _sel_right(gcum, eg_ref[...], 3)
    cb = _sel_right(beta, eb_ref[...], 2)
    rowb, rowg, dmat, a_low, egl = {}, {}, {}, {}, {}
    for p in pairs:
        ck, j = p
        cgp = cg[rs(ck), j * dh:(j + 1) * dh]
        cbp = cb[rs(ck), j * dh:(j + 1) * dh]
        rowg[p] = jnp.sum(jnp.where(eye2, cgp, 0.0), axis=0, keepdims=True)
        rowb[p] = jnp.sum(jnp.where(eye2, cbp, 0.0), axis=0, keepdims=True)
        dmat[p] = jnp.exp(jnp.where(causal2, cgp - rowg[p], NEG_BIG))
        a_low[p] = jnp.where(strict2, kk2[p] * dmat[p], 0.0) * cbp
        egl[p] = (jnp.exp(cgp), cgp[L - 1:L, :])

    q_bs, v_bs, lhs2, eg_full = [], [], {}, {}

    def do_q(j0, j1, after):
        for j in range(j0, j1):
            q = conv_cols(j * dh, after)
            q_bs.append((q * (lax.rsqrt(jnp.sum(q * q, axis=-1, keepdims=True) + EPS) * scale)).astype(BF16))

    def do_v(h0, h1, after):
        for h in range(h0, h1):
            v_bs.append(conv_cols(2 * key_dim + h * dh, after).astype(BF16))

    def do_lhs2(p0, p1, after):
        for p in pairs[p0:p1]:
            ck, j = p
            e, g_last = egl[p]
            qk2 = _dot_nt(q_bs[j][rs(ck)], kdup[p])
            attn2 = _split_halves((qk2 * dmat[p]).astype(BF16), lo)
            kt2 = jnp.concatenate([ks[j][rs(ck)]] * 2, axis=0).T
            kd2 = _split_halves((kt2 * jnp.exp(g_last - rowg[p])).astype(BF16), lo_kt)
            lhs2[p] = jnp.concatenate([attn2, kd2], axis=0)
            e_sw = pltpu.roll(e, L, 1)
            eg_full[p] = (jnp.where(lo, e, e_sw), jnp.where(lo, e_sw, e))

    n_p = len(pairs)
    fillers = [functools.partial(do_q, 0, n_qk // 2), functools.partial(do_q, n_qk // 2, n_qk),
               functools.partial(do_v, 0, n_v // 4), functools.partial(do_v, n_v // 4, n_v // 2),
               functools.partial(do_v, n_v // 2, 3 * n_v // 4), functools.partial(do_v, 3 * n_v // 4, n_v),
               functools.partial(do_lhs2, 0, n_p // 4), functools.partial(do_lhs2, n_p // 4, n_p // 2),
               functools.partial(do_lhs2, n_p // 2, 3 * n_p // 4), functools.partial(do_lhs2, 3 * n_p // 4, n_p)]
    t_list = _unit_lower_inverse_pairs([a_low[p] for p in pairs], L, fillers)
    t_inv = dict(zip(pairs, t_list))
    u2, w2 = {}, {}
    for p in pairs:
        ck, j = p
        tb = t_inv[p] * rowb[p]
        tbg = tb * jnp.exp(rowg[p])
        v_st = jnp.concatenate([v_bs[2 * j][rs(ck)], v_bs[2 * j + 1][rs(ck)]], axis=0)
        u2[p] = _dot(_split_halves(tb.astype(BF16), lo), v_st)
        w2[p] = _dot(_split_halves(tbg.astype(BF16), lo), kdup[p])

    for ck in range(nck):
        st = [state_ref[h] for h in range(n_v)]
        st_b = [s.astype(BF16) for s in st]
        m1 = []
        for h in range(n_v):
            j, i = h // 2, h % 2
            lhs1 = jnp.concatenate([w2[ck, j][i * L:(i + 1) * L].astype(BF16), q_bs[j][rs(ck)]], axis=0)
            m1.append(_dot(lhs1, st_b[h]))
        m2 = []
        for j in range(n_qk):
            vn = [u2[ck, j][i * L:(i + 1) * L] - m1[2 * j + i][:L] for i in range(2)]
            m2.append(_dot(lhs2[ck, j], jnp.concatenate(vn, axis=0).astype(BF16)))
        for h in range(n_v):
            j, i = h // 2, h % 2
            eg = eg_full[ck, j][i]
            state_ref[h] = st[h] * eg[L - 1:L, :] + m2[j][2 * L + i * dh:2 * L + (i + 1) * dh]
        for h in range(n_v):
            j, i = h // 2, h % 2
            o = m1[h][L:] * eg_full[ck, j][i] + m2[j][i * L:(i + 1) * L]
            o = _rms_rows(o, nw_ref[...]) * z_ref[rs(ck), h * dh:(h + 1) * dh].astype(F32)
            o_ref[rs(ck), h * dh:(h + 1) * dh] = o.astype(o_ref.dtype)


def _gdn_call(qkv, z, small, conv_w, dt_bias, a_log, norm_w, b_lane, a_lane):
    bsz, s, conv_dim = qkv.shape
    val_dim = z.shape[-1]
    key_dim = (conv_dim - val_dim) // 2
    n_v = val_dim // GDN_HEAD
    tb = min(GDN_TIME_BLOCK, s)
    assert s % tb == 0 and tb % GDN_CHUNK == 0
    place = lambda v, at: jnp.pad(v.astype(F32), (at, LANES - at - n_v)).reshape(1, LANES)
    head_of = (jnp.arange(n_v * GDN_CHUNK) // GDN_CHUNK)[None, :]
    lane = jnp.arange(LANES)[:, None]
    e_b = (lane == head_of + b_lane).astype(BF16)
    e_g = (lane == head_of + a_lane).astype(BF16)
    blk = lambda w: pl.BlockSpec((None, tb, w), lambda b, c: (b, c, 0))
    return pl.pallas_call(
        functools.partial(_gdn_kernel, key_dim=key_dim, b_lane=b_lane, a_lane=a_lane),
        grid=(bsz, s // tb),
        in_specs=[blk(conv_dim), blk(val_dim), blk(LANES),
                  _const_spec((CONV_K, conv_dim)),
                  _const_spec((1, LANES)), _const_spec((1, LANES)), _const_spec((1, GDN_HEAD)),
                  _const_spec(e_b.shape), _const_spec(e_g.shape)],
        out_specs=blk(val_dim),
        out_shape=jax.ShapeDtypeStruct((bsz, s, val_dim), BF16),
        scratch_shapes=[pltpu.VMEM((SUBLANES, conv_dim), F32),
                        pltpu.VMEM((n_v, GDN_HEAD, GDN_HEAD), F32)],
        compiler_params=_params(("parallel", "arbitrary")),
        name="gdn",
    )(qkv, z, small, conv_w.astype(F32), place(dt_bias, a_lane), place(a_log, a_lane),
      norm_w.astype(F32).reshape(1, GDN_HEAD), e_b, e_g)


def _mixout_kernel(x_ref, ys_ref, yg_ref, gt_ref, mod_ref, nw_ref, wsu_ref, wgu_ref, wo_ref, o_ref):
    d = x_ref.shape[1]
    up_s = _dot(ys_ref[...], wsu_ref[...])
    up_g = _dot(yg_ref[...], wgu_ref[...])
    gates = jax.nn.sigmoid(gt_ref[...].astype(F32))
    merged = gates[:, :d] * up_s + gates[:, d:] * up_g
    m2 = _dot(merged.astype(BF16), wo_ref[...])
    o_ref[...] = x_ref[...] + mod_ref[2:3, :] * _rms_rows(m2, nw_ref[...])


def _mixout_call(x, ys, yg, gates, mod3, norm_w, w_su, w_gu, w_o, tm):
    bsz, s, d = x.shape
    blk = lambda w: pl.BlockSpec((None, tm, w), lambda b, i: (b, i, 0))
    return pl.pallas_call(
        _mixout_kernel,
        grid=(bsz, s // tm),
        in_specs=[blk(d), blk(ys.shape[-1]), blk(yg.shape[-1]), blk(gates.shape[-1]),
                  pl.BlockSpec((None, 6, d), lambda b, i: (b, 0, 0)),
                  _const_spec((1, d)), _const_spec(w_su.shape), _const_spec(w_gu.shape),
                  _const_spec(w_o.shape)],
        out_specs=blk(d),
        out_shape=jax.ShapeDtypeStruct((bsz, s, d), F32),
        compiler_params=_params(("parallel", "parallel")),
        name="mix_out",
    )(x, ys, yg, gates, mod3, norm_w.reshape(1, d), w_su, w_gu, w_o)


MLP_HIDDEN_CHUNK = 1024


def _mlp_kernel(x_ref, mod_ref, nw1_ref, nw2_ref, wu_ref, wd_ref, o_ref):
    x = x_ref[...]
    h = _rms_rows(x, nw1_ref[...]) * (1.0 + mod_ref[4:5, :]) + mod_ref[3:4, :]
    hb = h.astype(BF16)
    hidden = wu_ref.shape[1]
    y = None
    for c0 in range(0, hidden, MLP_HIDDEN_CHUNK):
        up = _dot(hb, wu_ref[:, c0:c0 + MLP_HIDDEN_CHUNK])
        act = jnp.square(jnp.maximum(up, 0.0)).astype(BF16)
        part = _dot(act, wd_ref[c0:c0 + MLP_HIDDEN_CHUNK, :])
        y = part if y is None else y + part
    o_ref[...] = x + mod_ref[5:6, :] * _rms_rows(y, nw2_ref[...])


def _mlp_call(x, mod3, nw_pre, nw_post, w_up, w_down, tm):
    bsz, s, d = x.shape
    blk = pl.BlockSpec((None, tm, d), lambda b, i: (b, i, 0))
    return pl.pallas_call(
        _mlp_kernel,
        grid=(bsz, s // tm),
        in_specs=[blk, pl.BlockSpec((None, 6, d), lambda b, i: (b, 0, 0)),
                  _const_spec((1, d)), _const_spec((1, d)),
                  _const_spec(w_up.shape), _const_spec(w_down.shape)],
        out_specs=blk,
        out_shape=jax.ShapeDtypeStruct((bsz, s, d), F32),
        compiler_params=_params(("parallel", "parallel")),
        name="mlp",
    )(x, mod3, nw_pre.reshape(1, d), nw_post.reshape(1, d), w_up, w_down)


def _row_tile(s, want):
    t = min(want, s)
    while s % t:
        t //= 2
    return t


def kernel(x, c, w_ada, b_ada, norm_mix_pre, norm_mix_post, w_in, ssm_conv_w, ssm_conv_b, ssm_dt_bias, ssm_A_log, ssm_D, ssm_norm_w, gdn_conv_w, gdn_dt_bias, gdn_A_log, gdn_norm_w, w_ssm_up, w_gdn_up, w_out, norm_mlp_pre, norm_mlp_post, w_mlp_up, w_mlp_down):
    bsz, s, d = x.shape
    depth = w_in.shape[0]
    ssm_inner = w_ssm_up.shape[1]
    ssm_heads = ssm_dt_bias.shape[1]
    ssm_conv = ssm_conv_w.shape[2]
    gdn_conv = gdn_conv_w.shape[2]
    gdn_val = w_gdn_up.shape[1]
    gdn_heads = gdn_dt_bias.shape[1]
    assert s % SSM_CHUNK == 0 and s % GDN_CHUNK == 0
    sizes = (ssm_inner, ssm_conv, ssm_heads, gdn_conv, gdn_val, gdn_heads, gdn_heads, d, d)
    offs = [0]
    for w in sizes:
        offs.append(offs[-1] + w)
    assert offs[-1] == w_in.shape[2]
    b_lane = ssm_heads
    a_lane = ssm_heads + gdn_heads
    assert a_lane + gdn_heads <= LANES

    for l in range(depth):
        mod = _mod_call(c, w_ada[l], b_ada[l])
        mod3 = mod.reshape(bsz, 6, d)
        wl = w_in[l]
        seg = lambda i: wl[:, offs[i]:offs[i + 1]]
        w_small = jnp.pad(jnp.concatenate([seg(2), seg(5), seg(6)], axis=1),
                          ((0, 0), (0, LANES - a_lane - gdn_heads)))
        seg_ws = [seg(0), seg(1), seg(3), seg(4), wl[:, offs[7]:offs[9]], w_small]
        seg_ws = [w.astype(BF16) for w in seg_ws]
        kinds = ["silu", "conv", "plain", "silu", "plain", "plain"]
        convs = [(ssm_conv_w[l], ssm_conv_b[l])]
        z_ssm, xbc, qkv, z_gdn, gates, small = _inproj_call(
            x, mod3, norm_mix_pre[l], seg_ws, [BF16, BF16, BF16, BF16, BF16, F32], kinds, convs,
            _row_tile(s, 256))
        y_ssm = _ssd_call(z_ssm, xbc, small, ssm_dt_bias[l], ssm_A_log[l], ssm_D[l], ssm_norm_w[l])
        y_gdn = _gdn_call(qkv, z_gdn, small, gdn_conv_w[l], gdn_dt_bias[l], gdn_A_log[l],
                          gdn_norm_w[l], b_lane, a_lane)
        x = _mixout_call(x, y_ssm, y_gdn, gates, mod3, norm_mix_post[l],
                         w_ssm_up[l].astype(BF16), w_gdn_up[l].astype(BF16), w_out[l].astype(BF16),
                         _row_tile(s, 512))
        x = _mlp_call(x, mod3, norm_mlp_pre[l], norm_mlp_post[l],
                      w_mlp_up[l].astype(BF16), w_mlp_down[l].astype(BF16), _row_tile(s, 512))
    return x
```

```python
import functools

import jax
import jax.numpy as jnp
from jax import lax
from jax.experimental import pallas as pl
from jax.experimental.pallas import tpu as pltpu

F32 = jnp.float32
BF16 = jnp.bfloat16

EPS = 1e-6
CONV_K = 4
LANES = 128
SUBLANES = 8
NEG_BIG = -1e30

SSM_HEAD_DIM = 64
SSM_N_GROUPS = 8
SSM_D_STATE = 128
SSM_CHUNK = 128
GDN_HEAD = 128
GDN_CHUNK = 64

VMEM_LIMIT = 56 * 1024 * 1024


def _params(sem, vmem=VMEM_LIMIT):
    return pltpu.CompilerParams(dimension_semantics=sem, vmem_limit_bytes=vmem)


def _const_spec(shape):
    nd = len(shape)
    return pl.BlockSpec(shape, lambda *_: (0,) * nd, pipeline_mode=pl.Buffered(1))


def _dot(a, b):
    return jnp.dot(a, b, preferred_element_type=F32)


def _dot_nt(a, b):
    return lax.dot_general(a, b, (((1,), (1,)), ((), ())), preferred_element_type=F32)


def _dot_tn(a, b):
    return lax.dot_general(a, b, (((0,), (0,)), ((), ())), preferred_element_type=F32)


def _split(x, n):
    parts = []
    r = x
    for i in range(n):
        p = r.astype(BF16)
        parts.append(p)
        if i + 1 < n:
            r = r - p.astype(F32)
    return parts


def _sel_left(m01, x, n=3):
    out = None
    for p in reversed(_split(x, n)):
        t = _dot(m01, p)
        out = t if out is None else out + t
    return out


def _sel_right(x, m01, n=3):
    out = None
    for p in reversed(_split(x, n)):
        t = _dot(p, m01)
        out = t if out is None else out + t
    return out


def _silu(x):
    h = 0.5 * x
    return h + h * jnp.tanh(h)


def _softplus(x):
    return jnp.maximum(x, 0.0) + jnp.log1p(jnp.exp(-jnp.abs(x)))


def _rms_rows(x, w):
    return x * lax.rsqrt(jnp.mean(x * x, axis=-1, keepdims=True) + EPS) * w


def _tri_mask(n, strict=False):
    r = lax.broadcasted_iota(jnp.int32, (n, n), 0)
    c = lax.broadcasted_iota(jnp.int32, (n, n), 1)
    return (r > c) if strict else (r >= c)


def _conv_silu(x, prev, w, bias):
    rows, c = x.shape
    x3 = x.reshape(rows // SUBLANES, SUBLANES, c)
    row8 = lax.broadcasted_iota(jnp.int32, (1, SUBLANES, c), 1)
    acc = w[CONV_K - 1:CONV_K, :] * x3 + bias
    r, pr = x3, prev
    for j in range(1, CONV_K):
        r, pr = pltpu.roll(r, 1, 1), pltpu.roll(pr, 1, 0)
        before = jnp.concatenate([pr[None], r[:-1]], axis=0)
        acc = acc + w[CONV_K - 1 - j:CONV_K - j, :] * jnp.where(row8 < j, before, r)
    return _silu(acc).reshape(rows, c)


def _mod_kernel(c_ref, w_ref, b_ref, o_ref):
    ca = _silu(c_ref[...]).astype(BF16)
    o_ref[...] = _dot(ca, w_ref[...].astype(BF16)) + b_ref[...]


def _mod_call(c, w_ada, b_ada):
    bsz, d = c.shape
    n = w_ada.shape[1]
    return pl.pallas_call(
        _mod_kernel,
        grid=(n // d,),
        in_specs=[pl.BlockSpec((bsz, d), lambda j: (0, 0)),
                  pl.BlockSpec((d, d), lambda j: (0, j)),
                  pl.BlockSpec((1, d), lambda j: (0, j))],
        out_specs=pl.BlockSpec((bsz, d), lambda j: (0, j)),
        out_shape=jax.ShapeDtypeStruct((bsz, n), F32),
        compiler_params=_params(("arbitrary",)),
        name="mod",
    )(c, w_ada, b_ada.reshape(1, n))


IN_PROJ_COL_CHUNK = 256


def _inproj_kernel(x_ref, mod_ref, nw_ref, *refs, kinds):
    n_seg = len(kinds)
    n_conv = sum(k == "conv" for k in kinds)
    w_refs = refs[:n_seg]
    cw_refs = refs[n_seg:n_seg + 2 * n_conv]
    o_refs = refs[n_seg + 2 * n_conv:2 * n_seg + 2 * n_conv]
    hist_refs = refs[2 * n_seg + 2 * n_conv:]
    rows = x_ref.shape[0]

    @pl.when(pl.program_id(1) == 0)
    def _():
        for hist_ref in hist_refs:
            hist_ref[...] = jnp.zeros_like(hist_ref)

    h = _rms_rows(x_ref[...], nw_ref[...]) * (1.0 + mod_ref[1:2, :]) + mod_ref[0:1, :]
    hb = h.astype(BF16)
    heavy, light = [], []
    ci = 0
    for i, kind in enumerate(kinds):
        width = w_refs[i].shape[1]
        step = min(IN_PROJ_COL_CHUNK, width)
        chunks = [(i, ci, slice(c0, c0 + step)) for c0 in range(0, width, step)]
        (heavy if kind == "conv" else light).extend(chunks)
        ci += kind == "conv"
    tasks = []
    for n in range(max(len(heavy), len(light))):
        tasks += heavy[n:n + 1] + light[n:n + 1]

    def finish(task, y):
        i, ci, cs = task
        if kinds[i] == "conv":
            cw_ref, cb_ref, hist_ref = cw_refs[2 * ci], cw_refs[2 * ci + 1], hist_refs[ci]
            prev = hist_ref[:, cs]
            hist_ref[:, cs] = y[rows - SUBLANES:]
            y = _conv_silu(y, prev, cw_ref[:, cs], cb_ref[:, cs])
        elif kinds[i] == "silu":
            y = _silu(y)
        o_refs[i][:, cs] = y.astype(o_refs[i].dtype)

    pending = None
    for task in tasks:
        y = _dot(hb, w_refs[task[0]][:, task[2]])
        if pending is not None:
            finish(*pending)
        pending = (task, y)
    finish(*pending)


def _inproj_call(x, mod3, norm_w, seg_ws, seg_dtypes, kinds, convs, tm):
    bsz, s, d = x.shape
    in_specs = [pl.BlockSpec((None, tm, d), lambda b, i: (b, i, 0)),
                pl.BlockSpec((None, 6, d), lambda b, i: (b, 0, 0)),
                _const_spec((1, d))]
    in_specs += [_const_spec(w.shape) for w in seg_ws]
    conv_args = []
    for cw, cb in convs:
        conv_args += [cw.astype(F32), cb.astype(F32).reshape(1, -1)]
    in_specs += [_const_spec(a.shape) for a in conv_args]
    out_specs = [pl.BlockSpec((None, tm, w.shape[1]), lambda b, i: (b, i, 0)) for w in seg_ws]
    out_shape = [jax.ShapeDtypeStruct((bsz, s, w.shape[1]), dt) for w, dt in zip(seg_ws, seg_dtypes)]
    return pl.pallas_call(
        functools.partial(_inproj_kernel, kinds=tuple(kinds)),
        grid=(bsz, s // tm),
        in_specs=in_specs,
        out_specs=out_specs,
        out_shape=out_shape,
        scratch_shapes=[pltpu.VMEM((SUBLANES, cw.shape[1]), F32) for cw, _ in convs],
        compiler_params=_params(("parallel", "arbitrary")),
        name="in_proj",
    )(x, mod3, norm_w.reshape(1, d), *seg_ws, *conv_args)


def _ssd_kernel(z_ref, xbc_ref, sm_ref, dtb_ref, alog_ref, dskip_ref, nw_ref,
                e_ref, o_ref, state_ref, *, d_inner):
    L = SSM_CHUNK
    G = SSM_N_GROUPS
    N = SSM_D_STATE
    gw = d_inner // G
    hg = gw // SSM_HEAD_DIM

    @pl.when(pl.program_id(1) == 0)
    def _():
        state_ref[...] = jnp.zeros_like(state_ref)

    xs = xbc_ref[:, :d_inner].astype(F32)

    dt = _softplus(sm_ref[...] + dtb_ref[...])
    a = dt * (-jnp.exp(alog_ref[...]))
    tri = _tri_mask(L).astype(BF16)
    acum = _sel_left(tri, a)
    acum_t = acum.T
    a_last = acum[L - 1:L, :]
    causal = _tri_mask(L)
    lane_head = lax.broadcasted_iota(jnp.int32, (L, gw), 1) // SSM_HEAD_DIM

    groups = []
    for g in range(G):
        b_g = xbc_ref[:, d_inner + g * N: d_inner + (g + 1) * N]
        c_g = xbc_ref[:, d_inner + G * N + g * N: d_inner + G * N + (g + 1) * N]
        cb = _dot_nt(c_g, b_g)
        m_parts = []
        for j in range(hg):
            h = g * hg + j
            seg = acum[:, h:h + 1] - acum_t[h:h + 1, :]
            decay = jnp.exp(jnp.where(causal, seg, NEG_BIG))
            m_parts.append((cb * decay).astype(BF16))
        st = state_ref[g]
        groups.append((b_g, jnp.concatenate(m_parts, axis=1), st, _dot(c_g, st.astype(BF16))))

    e = e_ref[...]
    dt_e = _sel_right(dt, e, 1)
    eac_e = _sel_right(jnp.exp(acum), e, 1)
    ws_e = _sel_right(jnp.exp(a_last - acum), e, 1)
    ea_last_e = eac_e[L - 1:L, :]
    xdt = xs * dt_e
    xdt_b = xdt.astype(BF16)
    xw_b = (xdt * ws_e).astype(BF16)

    for g, (b_g, m_cat, st, y_off) in enumerate(groups):
        cs = slice(g * gw, (g + 1) * gw)
        xdt_g = xdt_b[:, cs]
        x_parts = [jnp.where(lane_head == j, xdt_g, jnp.zeros_like(xdt_g)) for j in range(hg)]
        y_diag = _dot(m_cat, jnp.concatenate(x_parts, axis=0))
        state_ref[g] = st * ea_last_e[:, cs] + _dot_tn(b_g, xw_b[:, cs])
        y = y_diag + y_off * eac_e[:, cs] + dskip_ref[:, cs] * xs[:, cs]
        y = y * z_ref[:, cs].astype(F32)
        o_ref[:, cs] = _rms_rows(y, nw_ref[:, cs]).astype(o_ref.dtype)


def _ssd_call(z, xbc, small, dt_bias, a_log, d_skip, norm_w):
    bsz, s, d_inner = z.shape
    conv_dim = xbc.shape[-1]
    n_heads = dt_bias.shape[0]
    L = SSM_CHUNK
    gw = d_inner // SSM_N_GROUPS
    pad = lambda v: jnp.pad(v.astype(F32), (0, LANES - n_heads)).reshape(1, LANES)
    expand = (jnp.arange(LANES)[:, None] == (jnp.arange(d_inner) // SSM_HEAD_DIM)[None, :]).astype(BF16)
    d_e = jnp.repeat(d_skip.astype(F32), SSM_HEAD_DIM).reshape(1, d_inner)
    blk = lambda w: pl.BlockSpec((None, L, w), lambda b, c: (b, c, 0))
    return pl.pallas_call(
        functools.partial(_ssd_kernel, d_inner=d_inner),
        grid=(bsz, s // L),
        in_specs=[blk(d_inner), blk(conv_dim), blk(LANES),
                  _const_spec((1, LANES)), _const_spec((1, LANES)),
                  _const_spec((1, d_inner)), _const_spec((1, d_inner)),
                  _const_spec((LANES, d_inner))],
        out_specs=blk(d_inner),
        out_shape=jax.ShapeDtypeStruct((bsz, s, d_inner), BF16),
        scratch_shapes=[pltpu.VMEM((SSM_N_GROUPS, SSM_D_STATE, gw), F32)],
        compiler_params=_params(("parallel", "arbitrary")),
        name="ssd",
    )(z, xbc, small, pad(dt_bias), pad(a_log), d_e, norm_w.astype(F32).reshape(1, d_inner), expand)


GDN_TIME_BLOCK = 256


def _zero_after(x):
    bits = pltpu.bitcast(x[:SUBLANES, :], jnp.uint32)
    bits = lax.shift_right_logical(lax.shift_right_logical(bits, jnp.uint32(16)), jnp.uint32(16))
    return pltpu.bitcast(bits, F32)[:1, :]


def _split_halves(x, lo):
    z = jnp.zeros_like(x)
    return jnp.concatenate([jnp.where(lo, x, z), jnp.where(lo, z, x)], axis=0)


def _unit_lower_inverse_pairs(a_list, n, fillers=()):
    fillers = list(fillers)
    r = lax.broadcasted_iota(jnp.int32, (n, 2 * n), 0)
    lane = lax.broadcasted_iota(jnp.int32, (n, 2 * n), 1)
    c = lane % n
    lo = lane < n
    eye = (r == c).astype(F32)
    invs = None
    b = 1
    while b < n:
        off = ((r // (2 * b)) == (c // (2 * b))) & ((r // b) % 2 == 1) & ((c // b) % 2 == 0)
        a_offs = [jnp.where(off, a, 0.0) for a in a_list]
        if b == 1:
            invs = [eye - a for a in a_offs]
        else:
            inv_b = [m.astype(BF16) for m in invs]
            ts = [_dot(mb, _split_halves(a.astype(BF16), lo)) for mb, a in zip(inv_b, a_offs)]
            if fillers:
                fillers.pop(0)(ts[-1])
            invs = [m - _dot(t.astype(BF16), _split_halves(mb, lo)) for m, t, mb in zip(invs, ts, inv_b)]
            if fillers:
                fillers.pop(0)(invs[-1])
        b *= 2
    for f in fillers:
        f(invs[-1])
    return invs


def _gdn_kernel(qkv_ref, z_ref, sm_ref, cw_ref, dtb_ref, alog_ref, nw_ref, eb_ref, eg_ref,
                o_ref, hist_ref, state_ref, *, key_dim, b_lane, a_lane):
    L = GDN_CHUNK
    dh = GDN_HEAD
    rows = qkv_ref.shape[0]
    nck = rows // L
    n_qk = key_dim // dh
    n_v = (qkv_ref.shape[1] - 2 * key_dim) // dh
    assert n_v == 2 * n_qk and dh == 2 * L

    @pl.when(pl.program_id(1) == 0)
    def _():
        hist_ref[...] = jnp.zeros_like(hist_ref)
        state_ref[...] = jnp.zeros_like(state_ref)

    def conv_cols(c0, after=None):
        cs = slice(c0, c0 + dh)
        raw = qkv_ref[:, cs].astype(F32)
        prev = hist_ref[:, cs]
        hist_ref[:, cs] = raw[rows - SUBLANES:]
        w = cw_ref[:, cs]
        if after is not None:
            w = w + _zero_after(after)
        return _conv_silu(raw, prev, w, 0.0)

    r = lax.broadcasted_iota(jnp.int32, (L, dh), 0)
    lane = lax.broadcasted_iota(jnp.int32, (L, dh), 1)
    c = lane % L
    lo = lane < L
    eye2, causal2, strict2 = r == c, r >= c, r > c
    lo_kt = lax.broadcasted_iota(jnp.int32, (dh, dh), 1) < L
    scale = dh ** -0.5
    rs = lambda ck: slice(ck * L, (ck + 1) * L)
    pairs = [(ck, j) for ck in range(nck) for j in range(n_qk)]

    sm = sm_ref[...]
    beta = jax.nn.sigmoid(sm)
    gl = -jnp.exp(alog_ref[...]) * _softplus(sm + dtb_ref[...])
    rr = lax.broadcasted_iota(jnp.int32, (rows, rows), 0)
    cc = lax.broadcasted_iota(jnp.int32, (rows, rows), 1)
    tri = ((rr >= cc) & (rr // L == cc // L)).astype(BF16)
    gcum = _sel_left(tri, gl)
    cg = _sel_right(gcum, eg_ref[...], 3)
    cb = _sel_right(beta, eb_ref[...], 1)
    ks, k_bs = [], []
    for j in range(n_qk):
        k = conv_cols(key_dim + j * dh)
        k = k * lax.rsqrt(jnp.sum(k * k, axis=-1, keepdims=True) + EPS)
        ks.append(k)
        k_bs.append(k.astype(BF16))
    kdup = {(ck, j): jnp.concatenate([k_bs[j][rs(ck)]] * 2, axis=0) for ck, j in pairs}
    kk2 = {p: _dot_nt(k_bs[p[1]][rs(p[0])], kdup[p]) for p in pairs}

    rowb, rowg, dmat, a_low, egl = {}, {}, {}, {}, {}
    for p in pairs:
        ck, j = p
        cgp = cg[rs(ck), j * dh:(j + 1) * dh]
        cbp = cb[rs(ck), j * dh:(j + 1) * dh]
        rowg[p] = jnp.sum(jnp.where(eye2, cgp, 0.0), axis=0, keepdims=True)
        rowb[p] = jnp.sum(jnp.where(eye2, cbp, 0.0), axis=0, keepdims=True)
        dmat[p] = jnp.exp(jnp.where(causal2, cgp - rowg[p], NEG_BIG))
        a_low[p] = jnp.where(strict2, kk2[p] * dmat[p], 0.0) * cbp
        egl[p] = (jnp.exp(cgp), cgp[L - 1:L, :])

    q_bs, v_bs, lhs2, eg_full = [], [], {}, {}

    def do_q(j0, j1, after):
        for j in range(j0, j1):
            q = conv_cols(j * dh)
            q_bs.append((q * (lax.rsqrt(jnp.sum(q * q, axis=-1, keepdims=True) + EPS) * scale)).astype(BF16))

    def do_v(h0, h1, after):
        for h in range(h0, h1):
            v_bs.append(conv_cols(2 * key_dim + h * dh, after).astype(BF16))

    def do_lhs2(p0, p1, after):
        for p in pairs[p0:p1]:
            ck, j = p
            e, g_last = egl[p]
            qk2 = _dot_nt(q_bs[j][rs(ck)], kdup[p])
            attn2 = _split_halves((qk2 * dmat[p]).astype(BF16), lo)
            kt2 = jnp.concatenate([ks[j][rs(ck)]] * 2, axis=0).T
            kd2 = _split_halves((kt2 * jnp.exp(g_last - rowg[p])).astype(BF16), lo_kt)
            lhs2[p] = jnp.concatenate([attn2, kd2], axis=0)
            e_sw = pltpu.roll(e, L, 1)
            eg_full[p] = (jnp.where(lo, e, e_sw), jnp.where(lo, e_sw, e))

    n_p = len(pairs)
    fillers = [functools.partial(do_q, 0, n_qk // 2), functools.partial(do_q, n_qk // 2, n_qk),
               functools.partial(do_v, 0, n_v // 4), functools.partial(do_v, n_v // 4, n_v // 2),
               functools.partial(do_v, n_v // 2, 3 * n_v // 4), functools.partial(do_v, 3 * n_v // 4, n_v),
               functools.partial(do_lhs2, 0, n_p // 4), functools.partial(do_lhs2, n_p // 4, n_p // 2),
               functools.partial(do_lhs2, n_p // 2, 3 * n_p // 4), functools.partial(do_lhs2, 3 * n_p // 4, n_p)]
    t_list = _unit_lower_inverse_pairs([a_low[p] for p in pairs], L, fillers)
    t_inv = dict(zip(pairs, t_list))
    u2, w2 = {}, {}
    for p in pairs:
        ck, j = p
        tb = t_inv[p] * rowb[p]
        tbg = tb * jnp.exp(rowg[p])
        v_st = jnp.concatenate([v_bs[2 * j][rs(ck)], v_bs[2 * j + 1][rs(ck)]], axis=0)
        u2[p] = _dot(_split_halves(tb.astype(BF16), lo), v_st)
        w2[p] = _dot(_split_halves(tbg.astype(BF16), lo), kdup[p])

    for ck in range(nck):
        st = [state_ref[h] for h in range(n_v)]
        st_b = [s.astype(BF16) for s in st]
        m1 = []
        for h in range(n_v):
            j, i = h // 2, h % 2
            lhs1 = jnp.concatenate([w2[ck, j][i * L:(i + 1) * L].astype(BF16), q_bs[j][rs(ck)]], axis=0)
            m1.append(_dot(lhs1, st_b[h]))
        m2 = []
        for j in range(n_qk):
            vn = [u2[ck, j][i * L:(i + 1) * L] - m1[2 * j + i][:L] for i in range(2)]
            m2.append(_dot(lhs2[ck, j], jnp.concatenate(vn, axis=0).astype(BF16)))
        for h in range(n_v):
            j, i = h // 2, h % 2
            eg = eg_full[ck, j][i]
            state_ref[h] = st[h] * eg[L - 1:L, :] + m2[j][2 * L + i * dh:2 * L + (i + 1) * dh]
        for h in range(n_v):
            j, i = h // 2, h % 2
            o = m1[h][L:] * eg_full[ck, j][i] + m2[j][i * L:(i + 1) * L]
            o = _rms_rows(o, nw_ref[...]) * z_ref[rs(ck), h * dh:(h + 1) * dh].astype(F32)
            o_ref[rs(ck), h * dh:(h + 1) * dh] = o.astype(o_ref.dtype)


def _gdn_call(qkv, z, small, conv_w, dt_bias, a_log, norm_w, b_lane, a_lane):
    bsz, s, conv_dim = qkv.shape
    val_dim = z.shape[-1]
    key_dim = (conv_dim - val_dim) // 2
    n_v = val_dim // GDN_HEAD
    tb = min(GDN_TIME_BLOCK, s)
    assert s % tb == 0 and tb % GDN_CHUNK == 0
    place = lambda v, at: jnp.pad(v.astype(F32), (at, LANES - at - n_v)).reshape(1, LANES)
    head_of = (jnp.arange(n_v * GDN_CHUNK) // GDN_CHUNK)[None, :]
    lane = jnp.arange(LANES)[:, None]
    e_b = (lane == head_of + b_lane).astype(BF16)
    e_g = (lane == head_of + a_lane).astype(BF16)
    blk = lambda w: pl.BlockSpec((None, tb, w), lambda b, c: (b, c, 0))
    return pl.pallas_call(
        functools.partial(_gdn_kernel, key_dim=key_dim, b_lane=b_lane, a_lane=a_lane),
        grid=(bsz, s // tb),
        in_specs=[blk(conv_dim), blk(val_dim), blk(LANES),
                  _const_spec((CONV_K, conv_dim)),
                  _const_spec((1, LANES)), _const_spec((1, LANES)), _const_spec((1, GDN_HEAD)),
                  _const_spec(e_b.shape), _const_spec(e_g.shape)],
        out_specs=blk(val_dim),
        out_shape=jax.ShapeDtypeStruct((bsz, s, val_dim), BF16),
        scratch_shapes=[pltpu.VMEM((SUBLANES, conv_dim), F32),
                        pltpu.VMEM((n_v, GDN_HEAD, GDN_HEAD), F32)],
        compiler_params=_params(("parallel", "arbitrary")),
        name="gdn",
    )(qkv, z, small, conv_w.astype(F32), place(dt_bias, a_lane), place(a_log, a_lane),
      norm_w.astype(F32).reshape(1, GDN_HEAD), e_b, e_g)


def _mixout_kernel(x_ref, ys_ref, yg_ref, gt_ref, mod_ref, nw_ref, wsu_ref, wgu_ref, wo_ref, o_ref):
    d = x_ref.shape[1]
    up_s = _dot(ys_ref[...], wsu_ref[...])
    up_g = _dot(yg_ref[...], wgu_ref[...])
    gates = jax.nn.sigmoid(gt_ref[...].astype(F32))
    merged = gates[:, :d] * up_s + gates[:, d:] * up_g
    m2 = _dot(merged.astype(BF16), wo_ref[...])
    o_ref[...] = x_ref[...] + mod_ref[2:3, :] * _rms_rows(m2, nw_ref[...])


def _mixout_call(x, ys, yg, gates, mod3, norm_w, w_su, w_gu, w_o, tm):
    bsz, s, d = x.shape
    blk = lambda w: pl.BlockSpec((None, tm, w), lambda b, i: (b, i, 0))
    return pl.pallas_call(
        _mixout_kernel,
        grid=(bsz, s // tm),
        in_specs=[blk(d), blk(ys.shape[-1]), blk(yg.shape[-1]), blk(gates.shape[-1]),
                  pl.BlockSpec((None, 6, d), lambda b, i: (b, 0, 0)),
                  _const_spec((1, d)), _const_spec(w_su.shape), _const_spec(w_gu.shape),
                  _const_spec(w_o.shape)],
        out_specs=blk(d),
        out_shape=jax.ShapeDtypeStruct((bsz, s, d), F32),
        compiler_params=_params(("parallel", "parallel")),
        name="mix_out",
    )(x, ys, yg, gates, mod3, norm_w.reshape(1, d), w_su, w_gu, w_o)


MLP_HIDDEN_CHUNK = 1024


def _mlp_kernel(x_ref, mod_ref, nw1_ref, nw2_ref, wu_ref, wd_ref, o_ref):
    x = x_ref[...]
    h = _rms_rows(x, nw1_ref[...]) * (1.0 + mod_ref[4:5, :]) + mod_ref[3:4, :]
    hb = h.astype(BF16)
    hidden = wu_ref.shape[1]
    y = None
    for c0 in range(0, hidden, MLP_HIDDEN_CHUNK):
        up = _dot(hb, wu_ref[:, c0:c0 + MLP_HIDDEN_CHUNK])
        act = jnp.square(jnp.maximum(up, 0.0)).astype(BF16)
        part = _dot(act, wd_ref[c0:c0 + MLP_HIDDEN_CHUNK, :])
        y = part if y is None else y + part
    o_ref[...] = x + mod_ref[5:6, :] * _rms_rows(y, nw2_ref[...])


def _mlp_call(x, mod3, nw_pre, nw_post, w_up, w_down, tm):
    bsz, s, d = x.shape
    blk = pl.BlockSpec((None, tm, d), lambda b, i: (b, i, 0))
    return pl.pallas_call(
        _mlp_kernel,
        grid=(bsz, s // tm),
        in_specs=[blk, pl.BlockSpec((None, 6, d), lambda b, i: (b, 0, 0)),
                  _const_spec((1, d)), _const_spec((1, d)),
                  _const_spec(w_up.shape), _const_spec(w_down.shape)],
        out_specs=blk,
        out_shape=jax.ShapeDtypeStruct((bsz, s, d), F32),
        compiler_params=_params(("parallel", "parallel")),
        name="mlp",
    )(x, mod3, nw_pre.reshape(1, d), nw_post.reshape(1, d), w_up, w_down)


def _row_tile(s, want):
    t = min(want, s)
    while s % t:
        t //= 2
    return t


def kernel(x, c, w_ada, b_ada, norm_mix_pre, norm_mix_post, w_in, ssm_conv_w, ssm_conv_b, ssm_dt_bias, ssm_A_log, ssm_D, ssm_norm_w, gdn_conv_w, gdn_dt_bias, gdn_A_log, gdn_norm_w, w_ssm_up, w_gdn_up, w_out, norm_mlp_pre, norm_mlp_post, w_mlp_up, w_mlp_down):
    bsz, s, d = x.shape
    depth = w_in.shape[0]
    ssm_inner = w_ssm_up.shape[1]
    ssm_heads = ssm_dt_bias.shape[1]
    ssm_conv = ssm_conv_w.shape[2]
    gdn_conv = gdn_conv_w.shape[2]
    gdn_val = w_gdn_up.shape[1]
    gdn_heads = gdn_dt_bias.shape[1]
    assert s % SSM_CHUNK == 0 and s % GDN_CHUNK == 0
    sizes = (ssm_inner, ssm_conv, ssm_heads, gdn_conv, gdn_val, gdn_heads, gdn_heads, d, d)
    offs = [0]
    for w in sizes:
        offs.append(offs[-1] + w)
    assert offs[-1] == w_in.shape[2]
    b_lane = ssm_heads
    a_lane = ssm_heads + gdn_heads
    assert a_lane + gdn_heads <= LANES

    for l in range(depth):
        mod = _mod_call(c, w_ada[l], b_ada[l])
        mod3 = mod.reshape(bsz, 6, d)
        wl = w_in[l]
        seg = lambda i: wl[:, offs[i]:offs[i + 1]]
        w_small = jnp.pad(jnp.concatenate([seg(2), seg(5), seg(6)], axis=1),
                          ((0, 0), (0, LANES - a_lane - gdn_heads)))
        seg_ws = [seg(0), seg(1), seg(3), seg(4), wl[:, offs[7]:offs[9]], w_small]
        seg_ws = [w.astype(BF16) for w in seg_ws]
        kinds = ["silu", "conv", "plain", "silu", "plain", "plain"]
        convs = [(ssm_conv_w[l], ssm_conv_b[l])]
        z_ssm, xbc, qkv, z_gdn, gates, small = _inproj_call(
            x, mod3, norm_mix_pre[l], seg_ws, [BF16, BF16, BF16, BF16, BF16, F32], kinds, convs,
            _row_tile(s, 256))
        y_ssm = _ssd_call(z_ssm, xbc, small, ssm_dt_bias[l], ssm_A_log[l], ssm_D[l], ssm_norm_w[l])
        y_gdn = _gdn_call(qkv, z_gdn, small, gdn_conv_w[l], gdn_dt_bias[l], gdn_A_log[l],
                          gdn_norm_w[l], b_lane, a_lane)
        x = _mixout_call(x, y_ssm, y_gdn, gates, mod3, norm_mix_post[l],
                         w_ssm_up[l].astype(BF16), w_gdn_up[l].astype(BF16), w_out[l].astype(BF16),
                         _row_tile(s, 512))
        x = _mlp_call(x, mod3, norm_mlp_pre[l], norm_mlp_post[l],
                      w_mlp_up[l].astype(BF16), w_mlp_down[l].astype(BF16), _row_tile(s, 512))
    return x
```
